```python
import math
import jax, jax.numpy as jnp
from jax import lax
import numpy as np

D_MODEL = 1024
BATCH = 32
SEQ = 2048
DEPTH = 1

GRID_W = 64
CTX_LEN = 256
HEAD_DIM = 64
DIFF_HEADS = 4
DIFF_V_DIM = 2 * HEAD_DIM
GQA_HEADS = 8
GQA_KV_HEADS = 2
MIX_WIDTH = DIFF_HEADS * DIFF_V_DIM + GQA_HEADS * HEAD_DIM
IN_SIZES = (DIFF_HEADS * 2 * HEAD_DIM, DIFF_HEADS * 2 * HEAD_DIM, DIFF_HEADS * DIFF_V_DIM,
            GQA_HEADS * HEAD_DIM, GQA_KV_HEADS * HEAD_DIM, GQA_KV_HEADS * HEAD_DIM)
IN_WIDTH = sum(IN_SIZES)
ROPE_THETA = 10000.0
Q_BLOCK = 128
N_EXPERTS = 64
N_GROUPS = 8
TOPK_GROUPS = 4
TOP_K = 8
EXPERT_DIM = 256
SHARED_DIM = 256
ROUTED_SCALE = 2.5
EXPERT_BLOCK = 256
N_MOD = 6
EPS = 1e-6

kernel_name = "hybrid_diffattn_gqa_moe_dit_block"


def rmsnorm(x, g):
    xf = x.astype(jnp.float32)
    y = xf * lax.rsqrt(jnp.mean(xf * xf, axis=-1, keepdims=True) + EPS)
    return (y * g.astype(jnp.float32)).astype(x.dtype)


def modulate(h, shift, scale):
    return h * (1.0 + scale) + shift


def rope_rotate(x, ang):
    q = ang.shape[-1]
    shape = (1, ang.shape[0]) + (1,) * (x.ndim - 3) + (q,)
    cos = jnp.cos(ang).reshape(shape).astype(x.dtype)
    sin = jnp.sin(ang).reshape(shape).astype(x.dtype)
    a, b = x[..., :q], x[..., q:]
    return jnp.concatenate([a * cos - b * sin, a * sin + b * cos], axis=-1)


def rope_2d(x, ang_row, ang_col):
    half = x.shape[-1] // 2
    return jnp.concatenate([rope_rotate(x[..., :half], ang_row),
                            rope_rotate(x[..., half:], ang_col)], axis=-1)


def attend(q, k, v, scale):
    b, s, hq, d = q.shape
    hk, dv = k.shape[2], v.shape[-1]
    g = hq // hk
    nblk = s // Q_BLOCK
    qb = q.reshape(b, nblk, Q_BLOCK, hk, g, d).swapaxes(0, 1)

    def one_block(qi):
        sc = jnp.einsum("bqhgd,bkhd->bhgqk", qi, k, preferred_element_type=jnp.float32) * scale
        p = jax.nn.softmax(sc, axis=-1).astype(v.dtype)
        return jnp.einsum("bhgqk,bkhe->bqhge", p, v)

    o = lax.map(one_block, qb)
    return o.swapaxes(0, 1).reshape(b, s, hq, dv)


def project(h, w_in, q_g, k_g):
    b, L, _ = h.shape
    p = h @ w_in
    offs = [int(o) for o in np.cumsum(IN_SIZES)[:-1]]
    dq, dk, dv, gq, gk, gv = jnp.split(p, offs, axis=-1)
    dq = dq.reshape(b, L, DIFF_HEADS, 2, HEAD_DIM)
    dk = dk.reshape(b, L, DIFF_HEADS, 2, HEAD_DIM)
    dv = dv.reshape(b, L, DIFF_HEADS, DIFF_V_DIM)
    gq = rmsnorm(gq.reshape(b, L, GQA_HEADS, HEAD_DIM), q_g)
    gk = rmsnorm(gk.reshape(b, L, GQA_KV_HEADS, HEAD_DIM), k_g)
    gv = gv.reshape(b, L, GQA_KV_HEADS, HEAD_DIM)
    return dq, dk, dv, gq, gk, gv


def mix_groups(dq, dk, dv, gq, gk, gv, lam, lam_init, subln_g):
    b, L = dq.shape[:2]
    sc = HEAD_DIM ** -0.5
    o1 = attend(dq[:, :, :, 0], dk[:, :, :, 0], dv, sc)
    o2 = attend(dq[:, :, :, 1], dk[:, :, :, 1], dv, sc)
    o_diff = rmsnorm(o1 - lam.astype(o2.dtype) * o2, subln_g) * (1.0 - lam_init)
    o_gqa = attend(gq, gk, gv, sc)
    return jnp.concatenate([o_diff.reshape(b, L, -1), o_gqa.reshape(b, L, -1)], axis=-1)


def swiglu(h, wg, wu, wd):
    return (jax.nn.silu(h @ wg) * (h @ wu)) @ wd


def moe(h, w_router, router_bias, w_eg, w_eu, w_ed, w_sg, w_su, w_sd):
    t, d = h.shape
    s = jax.nn.sigmoid(jnp.dot(h, w_router, preferred_element_type=jnp.float32))
    s_sel = s + router_bias.astype(jnp.float32)
    grp = s_sel.reshape(t, N_GROUPS, N_EXPERTS // N_GROUPS)
    grp_score = lax.top_k(grp, 2)[0].sum(-1)
    _, top_groups = lax.top_k(grp_score, TOPK_GROUPS)
    group_mask = jnp.any(top_groups[..., None] == jnp.arange(N_GROUPS), axis=1)
    expert_mask = jnp.repeat(group_mask, N_EXPERTS // N_GROUPS, axis=1)
    _, idx = lax.top_k(jnp.where(expert_mask, s_sel, -jnp.inf), TOP_K)
    w = jnp.take_along_axis(s, idx, axis=1)
    w = w / jnp.sum(w, axis=-1, keepdims=True) * ROUTED_SCALE

    n = t * TOP_K
    flat_e = idx.reshape(-1)
    flat_tok = jnp.repeat(jnp.arange(t, dtype=jnp.int32), TOP_K)
    flat_w = w.reshape(-1)
    order = jnp.argsort(flat_e)
    se, stok, sw = flat_e[order], flat_tok[order], flat_w[order]
    counts = jnp.bincount(flat_e, length=N_EXPERTS)
    start = jnp.cumsum(counts) - counts
    padded = (counts + EXPERT_BLOCK - 1) // EXPERT_BLOCK * EXPERT_BLOCK
    block_end = jnp.cumsum(padded)
    pstart = block_end - padded
    dest = pstart[se] + (jnp.arange(n) - start[se])
    n_blocks = (n + EXPERT_BLOCK - 1) // EXPERT_BLOCK + N_EXPERTS
    n_pad = n_blocks * EXPERT_BLOCK
    tok_buf = jnp.full((n_pad,), t, jnp.int32).at[dest].set(stok)
    w_buf = jnp.zeros((n_pad,), jnp.float32).at[dest].set(sw)
    block_expert = jnp.minimum(
        jnp.searchsorted(block_end, jnp.arange(n_blocks) * EXPERT_BLOCK, side="right"), N_EXPERTS - 1)
    h_pad = jnp.concatenate([h, jnp.zeros((1, d), h.dtype)], axis=0)

    def run_block(args):
        tok, wt, e = args
        y = swiglu(h_pad[tok], w_eg[e], w_eu[e], w_ed[e])
        return y * wt[:, None].astype(y.dtype)

    y_blocks = lax.map(run_block, (tok_buf.reshape(n_blocks, EXPERT_BLOCK),
                                   w_buf.reshape(n_blocks, EXPERT_BLOCK), block_expert))
    routed = jax.ops.segment_sum(y_blocks.reshape(n_pad, d), tok_buf, num_segments=t + 1)[:t]
    return routed + swiglu(h, w_sg, w_su, w_sd)


def setup_inputs(seed: int = 0) -> dict:
    key = jax.random.key(seed)
    ks = jax.random.split(key, 32)
    f32 = jnp.float32

    def nrm(k, shape, scale):
        return jax.random.normal(k, shape, f32) * scale

    def gain(k, shape):
        return 1.0 + 0.1 * jax.random.normal(k, shape, f32)

    L, D = DEPTH, D_MODEL
    return {
        "x": nrm(ks[0], (BATCH, SEQ, D), 1.0),
        "c": nrm(ks[1], (BATCH, D), 1.0),
        "ctx": nrm(ks[2], (BATCH, CTX_LEN, D), 1.0),
        "c_ctx": nrm(ks[3], (D,), 1.0),
        "w_mod": nrm(ks[4], (L, D, N_MOD * D), D ** -0.5),
        "b_mod": nrm(ks[5], (L, N_MOD * D), 0.02),
        "g_pre_mix": gain(ks[6], (L, D)),
        "g_post_mix": gain(ks[7], (L, D)),
        "w_in": nrm(ks[8], (L, D, IN_WIDTH), D ** -0.5),
        "q_norm_g": gain(ks[9], (L, HEAD_DIM)),
        "k_norm_g": gain(ks[10], (L, HEAD_DIM)),
        "lambda_q1": nrm(ks[11], (L, HEAD_DIM), 0.1),
        "lambda_k1": nrm(ks[12], (L, HEAD_DIM), 0.1),
        "lambda_q2": nrm(ks[13], (L, HEAD_DIM), 0.1),
        "lambda_k2": nrm(ks[14], (L, HEAD_DIM), 0.1),
        "subln_g": gain(ks[15], (L, DIFF_V_DIM)),
        "w_out": nrm(ks[16], (L, MIX_WIDTH, D), MIX_WIDTH ** -0.5),
        "g_pre_ffn": gain(ks[17], (L, D)),
        "g_post_ffn": gain(ks[18], (L, D)),
        "w_router": nrm(ks[19], (L, D, N_EXPERTS), D ** -0.5),
        "router_bias": nrm(ks[20], (L, N_EXPERTS), 0.01),
        "w_exp_gate": nrm(ks[21], (L, N_EXPERTS, D, EXPERT_DIM), D ** -0.5),
        "w_exp_up": nrm(ks[22], (L, N_EXPERTS, D, EXPERT_DIM), D ** -0.5),
        "w_exp_down": nrm(ks[23], (L, N_EXPERTS, EXPERT_DIM, D), EXPERT_DIM ** -0.5),
        "w_sh_gate": nrm(ks[24], (L, D, SHARED_DIM), D ** -0.5),
        "w_sh_up": nrm(ks[25], (L, D, SHARED_DIM), D ** -0.5),
        "w_sh_down": nrm(ks[26], (L, SHARED_DIM, D), SHARED_DIM ** -0.5),
    }


def reference(x, c, ctx, c_ctx, w_mod, b_mod, g_pre_mix, g_post_mix, w_in, q_norm_g, k_norm_g,
              lambda_q1, lambda_k1, lambda_q2, lambda_k2, subln_g, w_out, g_pre_ffn, g_post_ffn,
              w_router, router_bias, w_exp_gate, w_exp_up, w_exp_down, w_sh_gate, w_sh_up, w_sh_down):
    b, s, d = x.shape
    rows = s // GRID_W
    row_ids = jnp.repeat(jnp.arange(rows), GRID_W).astype(jnp.float32)
    col_ids = jnp.tile(jnp.arange(GRID_W), rows).astype(jnp.float32)
    n_freq = HEAD_DIM // 4
    inv_freq = ROPE_THETA ** (-jnp.arange(n_freq, dtype=jnp.float32) / n_freq)
    ang_row = row_ids[:, None] * inv_freq
    ang_col = col_ids[:, None] * inv_freq

    for l in range(DEPTH):
        lam_init = 0.8 - 0.6 * math.exp(-0.3 * l)
        lam = (jnp.exp(jnp.sum(lambda_q1[l].astype(jnp.float32) * lambda_k1[l].astype(jnp.float32)))
               - jnp.exp(jnp.sum(lambda_q2[l].astype(jnp.float32) * lambda_k2[l].astype(jnp.float32)))
               + lam_init)
        mod = jax.nn.silu(c) @ w_mod[l] + b_mod[l]
        mod_c = jax.nn.silu(c_ctx) @ w_mod[l] + b_mod[l]
        sh1, sc1, gt1, sh2, sc2, gt2 = jnp.split(mod[:, None, :], N_MOD, axis=-1)
        csh1, csc1, cgt1, csh2, csc2, cgt2 = jnp.split(mod_c, N_MOD)

        h = modulate(rmsnorm(x, g_pre_mix[l]), sh1, sc1)
        hc = modulate(rmsnorm(ctx, g_pre_mix[l]), csh1, csc1)
        dq, dk, dv, gq, gk, gv = project(h, w_in[l], q_norm_g[l], k_norm_g[l])
        dqc, dkc, dvc, gqc, gkc, gvc = project(hc, w_in[l], q_norm_g[l], k_norm_g[l])
        dq, dk = rope_2d(dq, ang_row, ang_col), rope_2d(dk, ang_row, ang_col)
        gq, gk = rope_2d(gq, ang_row, ang_col), rope_2d(gk, ang_row, ang_col)
        mixed = mix_groups(dq, jnp.concatenate([dkc, dk], axis=1), jnp.concatenate([dvc, dv], axis=1),
                           gq, jnp.concatenate([gkc, gk], axis=1), jnp.concatenate([gvc, gv], axis=1),
                           lam, lam_init, subln_g[l])
        x = x + gt1 * rmsnorm(mixed @ w_out[l], g_post_mix[l])

        h2 = modulate(rmsnorm(x, g_pre_ffn[l]), sh2, sc2)
        y = moe(h2.reshape(b * s, d), w_router[l], router_bias[l], w_exp_gate[l], w_exp_up[l],
                w_exp_down[l], w_sh_gate[l], w_sh_up[l], w_sh_down[l]).reshape(b, s, d)
        x = x + gt2 * rmsnorm(y, g_post_ffn[l])

        if l + 1 < DEPTH:
            mixed_c = mix_groups(dqc, dkc, dvc, gqc, gkc, gvc, lam, lam_init, subln_g[l])
            ctx = ctx + cgt1 * rmsnorm(mixed_c @ w_out[l], g_post_mix[l])
            h2c = modulate(rmsnorm(ctx, g_pre_ffn[l]), csh2, csc2)
            yc = moe(h2c.reshape(-1, d), w_router[l], router_bias[l], w_exp_gate[l], w_exp_up[l],
                     w_exp_down[l], w_sh_gate[l], w_sh_up[l], w_sh_down[l]).reshape(ctx.shape)
            ctx = ctx + cgt2 * rmsnorm(yc, g_post_ffn[l])
    return x
```

```python
import functools
import math

import jax
import jax.numpy as jnp
from jax import lax
from jax.experimental import pallas as pl
from jax.experimental.pallas import tpu as pltpu

HEAD_DIM = 64
DIFF_HEADS = 4
GQA_HEADS = 8
GQA_KV_HEADS = 2
GRID_W = 64
ROPE_THETA = 10000.0
N_GROUPS = 8
TOPK_GROUPS = 4
TOP_K = 8
ROUTED_SCALE = 2.5
N_MOD = 6
EPS = 1e-6
LAM_INIT = 0.8 - 0.6 * math.exp(-0.3 * 0)

LANES = 128
DQ_W = DIFF_HEADS * 2 * HEAD_DIM
DV_W = DIFF_HEADS * 2 * HEAD_DIM
GQ_W = GQA_HEADS * HEAD_DIM
GK_W = GQA_KV_HEADS * HEAD_DIM
OFF_DQ, OFF_DK, OFF_DV = 0, DQ_W, 2 * DQ_W
OFF_GQ = OFF_DV + DV_W
OFF_GK = OFF_GQ + GQ_W
OFF_GV = OFF_GK + GK_W
IN_WIDTH = OFF_GV + GK_W

VMEM_LIMIT = 56 * 1024 * 1024

f32 = jnp.float32
bf16 = jnp.bfloat16
NT_DIMS = (((1,), (1,)), ((), ()))


def _cparams(*sem):
    return pltpu.CompilerParams(dimension_semantics=sem, vmem_limit_bytes=VMEM_LIMIT)


def _rms(x):
    return x * lax.rsqrt(jnp.mean(x * x, axis=-1, keepdims=True) + EPS)


def _mod_kernel(c_ref, w_ref, b_ref, o_ref):
    o_ref[...] = jnp.dot(jax.nn.silu(c_ref[...]), w_ref[...], preferred_element_type=f32) + b_ref[...]


def _mod(cc, w_mod, b_mod):
    rows, d = cc.shape
    n = w_mod.shape[1]
    bn = d
    return pl.pallas_call(
        _mod_kernel,
        grid=(n // bn,),
        in_specs=[pl.BlockSpec((rows, d), lambda j: (0, 0)),
                  pl.BlockSpec((d, bn), lambda j: (0, j)),
                  pl.BlockSpec((1, bn), lambda j: (0, j))],
        out_specs=pl.BlockSpec((rows, bn), lambda j: (0, j)),
        out_shape=jax.ShapeDtypeStruct((rows, n), f32),
        compiler_params=_cparams("arbitrary"),
        name="mod",
    )(cc, w_mod, b_mod.reshape(1, n))


def _rope_tile(xt, cos, sin_signed, lane_lo):
    partner = jnp.where(lane_lo, pltpu.roll(xt, LANES - 16, 1), pltpu.roll(xt, 16, 1))
    return xt * cos + partner * sin_signed


def _inproj_kernel(x_ref, mod_ref, g_ref, w_ref, qg_ref, kg_ref, bd_ref, cos_ref, sin_ref, *out_refs,
                   latent):
    x = x_ref[0]
    h = _rms(x) * g_ref[...]
    h = h * (1.0 + mod_ref[0, 1:2, :]) + mod_ref[0, 0:1, :]
    hb = h.astype(bf16)
    tm = x.shape[0]

    def proj(off, width):
        return jnp.dot(hb, w_ref[:, off:off + width], preferred_element_type=f32)

    def head_rms(v, gain):
        w = v.shape[1]
        ss = jnp.dot((v * v).astype(bf16), bd_ref[0:w, 0:w], preferred_element_type=f32)
        return v * lax.rsqrt(ss * (1.0 / HEAD_DIM) + EPS) * gain

    if latent:
        dq_ref, gq_ref, dk_ref, dv_ref, gk_ref, gv_ref = out_refs
        cos, sin = cos_ref[...], sin_ref[...]
        lane_lo = (lax.broadcasted_iota(jnp.int32, (tm, LANES), 1) % 32) < 16
        rope = lambda v: _rope_tile(v, cos, sin, lane_lo)
    else:
        dk_ref, dv_ref, gk_ref, gv_ref = out_refs[-4:]
        rope = lambda v: v

    def store(ref, v, fn):
        for t in range(v.shape[1] // LANES):
            sl = slice(LANES * t, LANES * (t + 1))
            ref[0, :, sl] = fn(v[:, sl]).astype(ref.dtype)

    qscale = HEAD_DIM ** -0.5
    if latent:
        store(dq_ref, proj(OFF_DQ, DQ_W) * qscale, rope)
        store(gq_ref, head_rms(proj(OFF_GQ, GQ_W), qg_ref[...] * qscale), rope)
    store(dk_ref, proj(OFF_DK, DQ_W), rope)
    store(gk_ref, head_rms(proj(OFF_GK, GK_W), kg_ref[...]), rope)
    dv_ref[0] = proj(OFF_DV, DV_W).astype(bf16)
    gv_ref[0] = proj(OFF_GV, GK_W).astype(bf16)


def _inproj(x, mod, g_pre, w_in_b, qg, kg, bd, cos, sin, kv_bufs, *, latent, sk, row_off, tm):
    b, s, d = x.shape
    nt = s // tm
    kv_blk = row_off // tm
    kv_widths = (DQ_W, DV_W, GK_W, GK_W)
    const2 = lambda bi, i: (0, 0)
    in_specs = [
        pl.BlockSpec((1, tm, d), lambda bi, i: (bi, i, 0)),
        pl.BlockSpec((1, N_MOD, d), (lambda bi, i: (bi, 0, 0)) if latent else (lambda bi, i: (0, 0, 0))),
        pl.BlockSpec((1, d), const2),
        pl.BlockSpec((d, IN_WIDTH), const2),
        pl.BlockSpec((1, GQ_W), const2),
        pl.BlockSpec((1, GK_W), const2),
        pl.BlockSpec((GQ_W, GQ_W), const2),
        pl.BlockSpec((tm, LANES), (lambda bi, i: (i, 0)) if latent else const2),
        pl.BlockSpec((tm, LANES), (lambda bi, i: (i, 0)) if latent else const2),
    ]
    kv_specs = [pl.BlockSpec((1, tm, w), lambda bi, i: (bi, kv_blk + i, 0)) for w in kv_widths]
    kv_shapes = [jax.ShapeDtypeStruct((b, sk, w), bf16) for w in kv_widths]
    args = [x, mod, g_pre, w_in_b, qg, kg, bd, cos, sin]
    if latent:
        out_specs = [pl.BlockSpec((1, tm, DQ_W), lambda bi, i: (bi, i, 0)),
                     pl.BlockSpec((1, tm, GQ_W), lambda bi, i: (bi, i, 0))] + kv_specs
        out_shape = [jax.ShapeDtypeStruct((b, s, DQ_W), bf16), jax.ShapeDtypeStruct((b, s, GQ_W), bf16)] + kv_shapes
        aliases = {}
    else:
        in_specs += [pl.BlockSpec(memory_space=pl.ANY)] * 4
        args += list(kv_bufs)
        out_specs, out_shape = kv_specs, kv_shapes
        aliases = {len(args) - 4 + k: k for k in range(4)}
    return pl.pallas_call(
        functools.partial(_inproj_kernel, latent=latent),
        grid=(b, nt),
        in_specs=in_specs,
        out_specs=out_specs,
        out_shape=out_shape,
        input_output_aliases=aliases,
        compiler_params=_cparams("parallel", "arbitrary"),
        name="inproj_latent" if latent else "inproj_ctx",
    )(*args)


def _softmax_parts(s):
    m = jnp.max(s, axis=-1, keepdims=True)
    e = jnp.exp(s - m)
    return e, jnp.sum(e, axis=-1, keepdims=True)


def _attn_kernel(dq_ref, gq_ref, dk_ref, dv_ref, gk_ref, gv_ref, lam_ref, sg_ref, o_ref):
    tq = dq_ref.shape[1]
    lane = lax.broadcasted_iota(jnp.int32, (tq, LANES), 1)
    lo = lane < HEAD_DIM
    lv = lam_ref[...]
    lam = (jnp.exp(jnp.sum(lv[0:1] * lv[1:2], axis=1, keepdims=True))
           - jnp.exp(jnp.sum(lv[2:3] * lv[3:4], axis=1, keepdims=True)) + LAM_INIT)
    zero = jnp.zeros((), bf16)

    for h in range(DIFF_HEADS):
        sl = slice(LANES * h, LANES * (h + 1))
        qt = dq_ref[0, :, sl]
        kt = dk_ref[0, :, sl]
        s1 = lax.dot_general(jnp.where(lo, qt, zero), kt, NT_DIMS, preferred_element_type=f32)
        s2 = lax.dot_general(jnp.where(lo, zero, qt), kt, NT_DIMS, preferred_element_type=f32)
        e1, l1 = _softmax_parts(s1)
        e2, l2 = _softmax_parts(s2)
        dmat = e1 * (1.0 / l1) - e2 * (lam / l2)
        o = jnp.dot(dmat.astype(bf16), dv_ref[0, :, sl], preferred_element_type=f32)
        o = _rms(o) * (sg_ref[...] * (1.0 - LAM_INIT))
        o_ref[0, :, sl] = o.astype(o_ref.dtype)

    group = GQA_HEADS // GQA_KV_HEADS
    kt = gk_ref[0]
    vt = gv_ref[0]
    for t in range(GQA_HEADS // 2):
        halves = []
        for pos in range(2):
            hd = 2 * t + pos
            j = hd // group
            qt = gq_ref[0, :, LANES * t:LANES * (t + 1)]
            if pos != j:
                qt = pltpu.roll(qt, HEAD_DIM, 1)
            qm = jnp.where(lo if j == 0 else ~lo, qt, zero)
            s = lax.dot_general(qm, kt, NT_DIMS, preferred_element_type=f32)
            e, l = _softmax_parts(s)
            o = jnp.dot(e.astype(bf16), vt, preferred_element_type=f32) * (1.0 / l)
            if pos != j:
                o = pltpu.roll(o, HEAD_DIM, 1)
            halves.append(o)
        off = DIFF_HEADS * 2 * HEAD_DIM + LANES * t
        o_ref[0, :, off:off + LANES] = jnp.where(lo, halves[0], halves[1]).astype(o_ref.dtype)


def _attention(dq, gq, dk, dv, gk, gv, lamv, subln_g, *, tq):
    b, s, _ = dq.shape
    sk = dk.shape[1]
    qspec = lambda w: pl.BlockSpec((1, tq, w), lambda bi, qi: (bi, qi, 0))
    kspec = lambda w: pl.BlockSpec((1, sk, w), lambda bi, qi: (bi, 0, 0))
    const2 = lambda bi, qi: (0, 0)
    mix_w = DV_W + GQ_W
    return pl.pallas_call(
        _attn_kernel,
        grid=(b, s // tq),
        in_specs=[qspec(DQ_W), qspec(GQ_W), kspec(DQ_W), kspec(DV_W), kspec(GK_W), kspec(GK_W),
                  pl.BlockSpec((4, HEAD_DIM), const2), pl.BlockSpec((1, 2 * HEAD_DIM), const2)],
        out_specs=pl.BlockSpec((1, tq, mix_w), lambda bi, qi: (bi, qi, 0)),
        out_shape=jax.ShapeDtypeStruct((b, s, mix_w), bf16),
        compiler_params=_cparams("parallel", "arbitrary"),
        name="attn",
    )(dq, gq, dk, dv, gk, gv, lamv, subln_g)


def _outproj_kernel(mix_ref, x_ref, mod_ref, wo_ref, gpost_ref, gpre_ref, wr_ref, x1_ref, h2_ref, st_ref):
    o = jnp.dot(mix_ref[0], wo_ref[...], preferred_element_type=f32)
    x1 = x_ref[0] + mod_ref[0, 2:3, :] * (_rms(o) * gpost_ref[...])
    x1_ref[0] = x1
    h2 = _rms(x1) * gpre_ref[...]
    h2 = (h2 * (1.0 + mod_ref[0, 4:5, :]) + mod_ref[0, 3:4, :]).astype(bf16)
    h2_ref[0] = h2
    st_ref[...] = jax.nn.sigmoid(lax.dot_general(wr_ref[...], h2, NT_DIMS, preferred_element_type=f32))


def _outproj(mixed, x, mod, w_out_b, g_post, g_pre, wr_t, *, tm):
    b, s, d = x.shape
    nt = s // tm
    ne = wr_t.shape[0]
    mw = mixed.shape[2]
    const2 = lambda bi, i: (0, 0)
    tile = lambda w: pl.BlockSpec((1, tm, w), lambda bi, i: (bi, i, 0))
    return pl.pallas_call(
        _outproj_kernel,
        grid=(b, nt),
        in_specs=[tile(mw), tile(d), pl.BlockSpec((1, N_MOD, d), lambda bi, i: (bi, 0, 0)),
                  pl.BlockSpec((mw, d), const2), pl.BlockSpec((1, d), const2), pl.BlockSpec((1, d), const2),
                  pl.BlockSpec((ne, d), const2)],
        out_specs=[tile(d), tile(d), pl.BlockSpec((ne, tm), lambda bi, i: (0, bi * nt + i))],
        out_shape=[jax.ShapeDtypeStruct((b, s, d), f32), jax.ShapeDtypeStruct((b, s, d), bf16),
                   jax.ShapeDtypeStruct((ne, b * s), f32)],
        compiler_params=_cparams("parallel", "arbitrary"),
        name="outproj",
    )(mixed, x, mod, w_out_b, g_post, g_pre, wr_t)


def _first_argmax(v, iota, n):
    m = jnp.max(v, axis=0, keepdims=True)
    idx = jnp.min(jnp.where(v == m, iota, n), axis=0, keepdims=True)
    return m, idx


def _route_gates_t(s, bias):
    ne, tm = s.shape
    gsz = ne // N_GROUPS
    neg = -jnp.inf
    ssel = s + bias
    sub = lax.broadcasted_iota(jnp.int32, (gsz, tm), 0).astype(f32)
    gscore = []
    for g in range(N_GROUPS):
        blk = ssel[gsz * g:gsz * (g + 1)]
        m1, i1 = _first_argmax(blk, sub, float(gsz))
        m2 = jnp.max(jnp.where(sub == i1, neg, blk), axis=0, keepdims=True)
        gscore.append(m1 + m2)
    gs = jnp.concatenate(gscore, axis=0)
    giota = lax.broadcasted_iota(jnp.int32, (N_GROUPS, tm), 0).astype(f32)
    gsel = jnp.zeros((N_GROUPS, tm), f32)
    for _ in range(TOPK_GROUPS):
        _, gi = _first_argmax(gs, giota, float(N_GROUPS))
        hit = giota == gi
        gsel = jnp.where(hit, 1.0, gsel)
        gs = jnp.where(hit, neg, gs)
    emask = jnp.concatenate([jnp.broadcast_to(gsel[g:g + 1], (gsz, tm)) for g in range(N_GROUPS)], axis=0)
    cand = jnp.where(emask > 0.0, ssel, neg)
    eiota = lax.broadcasted_iota(jnp.int32, (ne, tm), 0).astype(f32)
    w = jnp.zeros((ne, tm), f32)
    for _ in range(TOP_K):
        _, ei = _first_argmax(cand, eiota, float(ne))
        hit = eiota == ei
        w = jnp.where(hit, s, w)
        cand = jnp.where(hit, neg, cand)
    return w / jnp.sum(w, axis=0, keepdims=True) * ROUTED_SCALE


def _route_kernel(st_ref, bias_ref, g_ref):
    gt = _route_gates_t(st_ref[...], bias_ref[...])
    ne, tm = gt.shape
    gt = jnp.concatenate([gt, jnp.zeros((LANES - ne, tm), f32)], axis=0)
    g_ref[...] = gt.T


def _route(st, bias, *, tm):
    ne, t = st.shape
    return pl.pallas_call(
        _route_kernel,
        grid=(t // tm,),
        in_specs=[pl.BlockSpec((ne, tm), lambda i: (0, i)), pl.BlockSpec((ne, 1), lambda i: (0, 0))],
        out_specs=pl.BlockSpec((tm, LANES), lambda i: (i, 0)),
        out_shape=jax.ShapeDtypeStruct((t, LANES), f32),
        compiler_params=_cparams("parallel"),
        name="route",
    )(st, bias)


def _swiglu_hidden(h, wg, wu):
    return jax.nn.silu(jnp.dot(h, wg, preferred_element_type=f32)) * jnp.dot(h, wu, preferred_element_type=f32)


def _moe_kernel(h_ref, gate_ref, weg_ref, weu_ref, wed_ref, wsg_ref, wsu_ref, wsd_ref, x1_ref, mod_ref, gp_ref,
                o_ref, acc_ref):
    e = pl.program_id(1)
    h = h_ref[...]

    @pl.when(e == 0)
    def _():
        a = _swiglu_hidden(h, wsg_ref[...], wsu_ref[...])
        acc_ref[...] = jnp.dot(a.astype(bf16), wsd_ref[...], preferred_element_type=f32)

    gates = gate_ref[...]
    lane = lax.broadcasted_iota(jnp.int32, gates.shape, 1)
    g = jnp.sum(jnp.where(lane == e, gates, 0.0), axis=1, keepdims=True)
    a = _swiglu_hidden(h, weg_ref[0], weu_ref[0]) * g
    acc_ref[...] += jnp.dot(a.astype(bf16), wed_ref[0], preferred_element_type=f32)

    @pl.when(e == pl.num_programs(1) - 1)
    def _():
        o_ref[...] = x1_ref[...] + mod_ref[0, 5:6, :] * (_rms(acc_ref[...]) * gp_ref[...])


def _moe(h2, gates, weg, weu, wed, wsg, wsu, wsd, x1, mod, g_post, *, tm, s):
    t, d = h2.shape
    ne, _, de = weg.shape
    ds = wsg.shape[1]
    per_b = s // tm
    tile = lambda w: pl.BlockSpec((tm, w), lambda i, e: (i, 0))
    const2 = lambda i, e: (0, 0)
    return pl.pallas_call(
        _moe_kernel,
        grid=(t // tm, ne),
        in_specs=[tile(d), tile(LANES),
                  pl.BlockSpec((1, d, de), lambda i, e: (e, 0, 0)),
                  pl.BlockSpec((1, d, de), lambda i, e: (e, 0, 0)),
                  pl.BlockSpec((1, de, d), lambda i, e: (e, 0, 0)),
                  pl.BlockSpec((d, ds), const2), pl.BlockSpec((d, ds), const2), pl.BlockSpec((ds, d), const2),
                  tile(d), pl.BlockSpec((1, N_MOD, d), lambda i, e: (i // per_b, 0, 0)),
                  pl.BlockSpec((1, d), const2)],
        out_specs=tile(d),
        out_shape=jax.ShapeDtypeStruct((t, d), f32),
        scratch_shapes=[pltpu.VMEM((tm, d), f32)],
        compiler_params=_cparams("parallel", "arbitrary"),
        name="moe",
    )(h2, gates, weg, weu, wed, wsg, wsu, wsd, x1, mod, g_post)


def _tiles(s, sc):
    tm = math.gcd(s, 512)
    assert sc % tm == 0 or tm % sc == 0
    return dict(tm=tm, tm_ctx=math.gcd(sc, tm), tq=math.gcd(s, 256), tr=math.gcd(s, 2048), tmoe=math.gcd(s, 1024))


def _rope_tables(s):
    n_freq = HEAD_DIM // 4
    inv_freq = ROPE_THETA ** (-jnp.arange(n_freq, dtype=f32) / n_freq)
    pos = jnp.arange(s)
    ang_r = (pos // GRID_W).astype(f32)[:, None] * inv_freq
    ang_c = (pos % GRID_W).astype(f32)[:, None] * inv_freq
    cr, sr, cc, sc_ = jnp.cos(ang_r), jnp.sin(ang_r), jnp.cos(ang_c), jnp.sin(ang_c)
    cos = jnp.concatenate([cr, cr, cc, cc] * (LANES // HEAD_DIM), axis=1)
    sin = jnp.concatenate([-sr, sr, -sc_, sc_] * (LANES // HEAD_DIM), axis=1)
    return cos, sin


def kernel(x, c, ctx, c_ctx, w_mod, b_mod, g_pre_mix, g_post_mix, w_in, q_norm_g, k_norm_g, lambda_q1, lambda_k1,
           lambda_q2, lambda_k2, subln_g, w_out, g_pre_ffn, g_post_ffn, w_router, router_bias, w_exp_gate, w_exp_up,
           w_exp_down, w_sh_gate, w_sh_up, w_sh_down):
    assert w_mod.shape[0] == 1, "single-layer block"
    b, s, d = x.shape
    sc = ctx.shape[1]
    sk = s + sc
    ne = w_router.shape[2]
    assert w_in.shape[2] == IN_WIDTH and ne % N_GROUPS == 0 and ne <= LANES
    tl = _tiles(s, sc)

    pad = (-(b + 1)) % 8
    cc = jnp.concatenate([c, c_ctx[None], jnp.zeros((pad, d), f32)], axis=0)
    mod_all = _mod(cc, w_mod[0], b_mod[0])
    mod = mod_all[:b].reshape(b, N_MOD, d)
    mod_c = mod_all[b:b + 1].reshape(1, N_MOD, d)

    w_in_b = w_in[0].astype(bf16)
    qg = jnp.tile(q_norm_g[0], GQA_HEADS)[None]
    kg = jnp.tile(k_norm_g[0], GQA_KV_HEADS)[None]
    head_id = jnp.arange(GQ_W) // HEAD_DIM
    bd = (head_id[:, None] == head_id[None, :]).astype(bf16)
    cos, sin = _rope_tables(s)
    g_pre = g_pre_mix[0][None]

    dq, gq, dk, dv, gk, gv = _inproj(x, mod, g_pre, w_in_b, qg, kg, bd, cos, sin, None,
                                     latent=True, sk=sk, row_off=0, tm=tl["tm"])
    dk, dv, gk, gv = _inproj(ctx, mod_c, g_pre, w_in_b, qg, kg, bd, cos, sin, (dk, dv, gk, gv),
                             latent=False, sk=sk, row_off=s, tm=tl["tm_ctx"])

    lamv = jnp.stack([lambda_q1[0], lambda_k1[0], lambda_q2[0], lambda_k2[0]])
    mixed = _attention(dq, gq, dk, dv, gk, gv, lamv, subln_g[0][None], tq=tl["tq"])

    x1, h2, st = _outproj(mixed, x, mod, w_out[0].astype(bf16), g_post_mix[0][None], g_pre_ffn[0][None],
                          w_router[0].T.astype(bf16), tm=tl["tm"])
    gates = _route(st, router_bias[0][:, None], tm=tl["tr"])

    out = _moe(h2.reshape(b * s, d), gates,
               w_exp_gate[0].astype(bf16), w_exp_up[0].astype(bf16), w_exp_down[0].astype(bf16),
               w_sh_gate[0].astype(bf16), w_sh_up[0].astype(bf16), w_sh_down[0].astype(bf16),
               x1.reshape(b * s, d), mod, g_post_ffn[0][None], tm=tl["tmoe"], s=s)
    return out.reshape(b, s, d)
```

```python
import functools
import math

import jax
import jax.numpy as jnp
from jax import lax
from jax.experimental import pallas as pl
from jax.experimental.pallas import tpu as pltpu

HEAD_DIM = 64
DIFF_HEADS = 4
GQA_HEADS = 8
GQA_KV_HEADS = 2
GRID_W = 64
ROPE_THETA = 10000.0
N_GROUPS = 8
TOPK_GROUPS = 4
TOP_K = 8
ROUTED_SCALE = 2.5
N_MOD = 6
EPS = 1e-6
LAM_INIT = 0.8 - 0.6 * math.exp(-0.3 * 0)

LANES = 128
DQ_W = DIFF_HEADS * 2 * HEAD_DIM
DV_W = DIFF_HEADS * 2 * HEAD_DIM
GQ_W = GQA_HEADS * HEAD_DIM
GK_W = GQA_KV_HEADS * HEAD_DIM
OFF_DQ, OFF_DK, OFF_DV = 0, DQ_W, 2 * DQ_W
OFF_GQ = OFF_DV + DV_W
OFF_GK = OFF_GQ + GQ_W
OFF_GV = OFF_GK + GK_W
IN_WIDTH = OFF_GV + GK_W

VMEM_LIMIT = 56 * 1024 * 1024

N_EXPERTS_MAX = 64
SUB = 256
ROW_ALIGN = 16
GMM_BLOCK = 256
NLOC = -(-(SUB * TOP_K + N_EXPERTS_MAX * (ROW_ALIGN - 1)) // SUB) * SUB


def _swiglu_hidden(h, wg, wu):
    return jax.nn.silu(jnp.dot(h, wg, preferred_element_type=f32)) * jnp.dot(h, wu, preferred_element_type=f32)

f32 = jnp.float32
bf16 = jnp.bfloat16
NT_DIMS = (((1,), (1,)), ((), ()))


def _cparams(*sem):
    return pltpu.CompilerParams(dimension_semantics=sem, vmem_limit_bytes=VMEM_LIMIT)


def _rms(x):
    return x * lax.rsqrt(jnp.mean(x * x, axis=-1, keepdims=True) + EPS)


def _mod_kernel(c_ref, w_ref, b_ref, o_ref):
    o_ref[...] = jnp.dot(jax.nn.silu(c_ref[...]), w_ref[...], preferred_element_type=f32) + b_ref[...]


def _mod(cc, w_mod, b_mod):
    rows, d = cc.shape
    n = w_mod.shape[1]
    bn = d
    return pl.pallas_call(
        _mod_kernel,
        grid=(n // bn,),
        in_specs=[pl.BlockSpec((rows, d), lambda j: (0, 0)),
                  pl.BlockSpec((d, bn), lambda j: (0, j)),
                  pl.BlockSpec((1, bn), lambda j: (0, j))],
        out_specs=pl.BlockSpec((rows, bn), lambda j: (0, j)),
        out_shape=jax.ShapeDtypeStruct((rows, n), f32),
        compiler_params=_cparams("arbitrary"),
        name="mod",
    )(cc, w_mod, b_mod.reshape(1, n))


def _rope_tile(xt, cos, sin_signed, lane_lo):
    partner = jnp.where(lane_lo, pltpu.roll(xt, LANES - 16, 1), pltpu.roll(xt, 16, 1))
    return xt * cos + partner * sin_signed


def _inproj_kernel(x_ref, mod_ref, g_ref, w_ref, qg_ref, kg_ref, bd_ref, cos_ref, sin_ref, *out_refs,
                   latent):
    x = x_ref[0]
    h = _rms(x) * g_ref[...]
    h = h * (1.0 + mod_ref[0, 1:2, :]) + mod_ref[0, 0:1, :]
    hb = h.astype(bf16)
    tm = x.shape[0]

    def proj(off, width):
        return jnp.dot(hb, w_ref[:, off:off + width], preferred_element_type=f32)

    def head_rms(v, gain):
        w = v.shape[1]
        ss = jnp.dot((v * v).astype(bf16), bd_ref[0:w, 0:w], preferred_element_type=f32)
        return v * lax.rsqrt(ss * (1.0 / HEAD_DIM) + EPS) * gain

    if latent:
        dq_ref, gq_ref, dk_ref, dv_ref, gk_ref, gv_ref = out_refs
        cos, sin = cos_ref[...], sin_ref[...]
        lane_lo = (lax.broadcasted_iota(jnp.int32, (tm, LANES), 1) % 32) < 16
        rope = lambda v: _rope_tile(v, cos, sin, lane_lo)
    else:
        dk_ref, dv_ref, gk_ref, gv_ref = out_refs[-4:]
        rope = lambda v: v

    def store(ref, v, fn):
        for t in range(v.shape[1] // LANES):
            sl = slice(LANES * t, LANES * (t + 1))
            ref[0, :, sl] = fn(v[:, sl]).astype(ref.dtype)

    qscale = HEAD_DIM ** -0.5
    if latent:
        store(dq_ref, proj(OFF_DQ, DQ_W) * qscale, rope)
        store(gq_ref, head_rms(proj(OFF_GQ, GQ_W), qg_ref[...] * qscale), rope)
    store(dk_ref, proj(OFF_DK, DQ_W), rope)
    store(gk_ref, head_rms(proj(OFF_GK, GK_W), kg_ref[...]), rope)
    dv_ref[0] = proj(OFF_DV, DV_W).astype(bf16)
    gv_ref[0] = proj(OFF_GV, GK_W).astype(bf16)


def _inproj(x, mod, g_pre, w_in_b, qg, kg, bd, cos, sin, kv_bufs, *, latent, sk, row_off, tm):
    b, s, d = x.shape
    nt = s // tm
    kv_blk = row_off // tm
    kv_widths = (DQ_W, DV_W, GK_W, GK_W)
    const2 = lambda bi, i: (0, 0)
    in_specs = [
        pl.BlockSpec((1, tm, d), lambda bi, i: (bi, i, 0)),
        pl.BlockSpec((1, N_MOD, d), (lambda bi, i: (bi, 0, 0)) if latent else (lambda bi, i: (0, 0, 0))),
        pl.BlockSpec((1, d), const2),
        pl.BlockSpec((d, IN_WIDTH), const2),
        pl.BlockSpec((1, GQ_W), const2),
        pl.BlockSpec((1, GK_W), const2),
        pl.BlockSpec((GQ_W, GQ_W), const2),
        pl.BlockSpec((tm, LANES), (lambda bi, i: (i, 0)) if latent else const2),
        pl.BlockSpec((tm, LANES), (lambda bi, i: (i, 0)) if latent else const2),
    ]
    kv_specs = [pl.BlockSpec((1, tm, w), lambda bi, i: (bi, kv_blk + i, 0)) for w in kv_widths]
    kv_shapes = [jax.ShapeDtypeStruct((b, sk, w), bf16) for w in kv_widths]
    args = [x, mod, g_pre, w_in_b, qg, kg, bd, cos, sin]
    if latent:
        out_specs = [pl.BlockSpec((1, tm, DQ_W), lambda bi, i: (bi, i, 0)),
                     pl.BlockSpec((1, tm, GQ_W), lambda bi, i: (bi, i, 0))] + kv_specs
        out_shape = [jax.ShapeDtypeStruct((b, s, DQ_W), bf16), jax.ShapeDtypeStruct((b, s, GQ_W), bf16)] + kv_shapes
        aliases = {}
    else:
        in_specs += [pl.BlockSpec(memory_space=pl.ANY)] * 4
        args += list(kv_bufs)
        out_specs, out_shape = kv_specs, kv_shapes
        aliases = {len(args) - 4 + k: k for k in range(4)}
    return pl.pallas_call(
        functools.partial(_inproj_kernel, latent=latent),
        grid=(b, nt),
        in_specs=in_specs,
        out_specs=out_specs,
        out_shape=out_shape,
        input_output_aliases=aliases,
        compiler_params=_cparams("parallel", "arbitrary"),
        name="inproj_latent" if latent else "inproj_ctx",
    )(*args)


def _softmax_parts(s):
    m = jnp.max(s, axis=-1, keepdims=True)
    e = jnp.exp(s - m)
    return e, jnp.sum(e, axis=-1, keepdims=True)


def _attn_kernel(dq_ref, gq_ref, dk_ref, dv_ref, gk_ref, gv_ref, lam_ref, sg_ref, o_ref):
    tq = dq_ref.shape[1]
    lane = lax.broadcasted_iota(jnp.int32, (tq, LANES), 1)
    lo = lane < HEAD_DIM
    lv = lam_ref[...]
    lam = (jnp.exp(jnp.sum(lv[0:1] * lv[1:2], axis=1, keepdims=True))
           - jnp.exp(jnp.sum(lv[2:3] * lv[3:4], axis=1, keepdims=True)) + LAM_INIT)
    zero = jnp.zeros((), bf16)

    for h in range(DIFF_HEADS):
        sl = slice(LANES * h, LANES * (h + 1))
        qt = dq_ref[0, :, sl]
        kt = dk_ref[0, :, sl]
        s1 = lax.dot_general(jnp.where(lo, qt, zero), kt, NT_DIMS, preferred_element_type=f32)
        s2 = lax.dot_general(jnp.where(lo, zero, qt), kt, NT_DIMS, preferred_element_type=f32)
        e1, l1 = _softmax_parts(s1)
        e2, l2 = _softmax_parts(s2)
        dmat = e1 * (1.0 / l1) - e2 * (lam / l2)
        o = jnp.dot(dmat.astype(bf16), dv_ref[0, :, sl], preferred_element_type=f32)
        o = _rms(o) * (sg_ref[...] * (1.0 - LAM_INIT))
        o_ref[0, :, sl] = o.astype(o_ref.dtype)

    group = GQA_HEADS // GQA_KV_HEADS
    kt = gk_ref[0]
    vt = gv_ref[0]
    for t in range(GQA_HEADS // 2):
        halves = []
        for pos in range(2):
            hd = 2 * t + pos
            j = hd // group
            qt = gq_ref[0, :, LANES * t:LANES * (t + 1)]
            if pos != j:
                qt = pltpu.roll(qt, HEAD_DIM, 1)
            qm = jnp.where(lo if j == 0 else ~lo, qt, zero)
            s = lax.dot_general(qm, kt, NT_DIMS, preferred_element_type=f32)
            e, l = _softmax_parts(s)
            o = jnp.dot(e.astype(bf16), vt, preferred_element_type=f32) * (1.0 / l)
            if pos != j:
                o = pltpu.roll(o, HEAD_DIM, 1)
            halves.append(o)
        off = DIFF_HEADS * 2 * HEAD_DIM + LANES * t
        o_ref[0, :, off:off + LANES] = jnp.where(lo, halves[0], halves[1]).astype(o_ref.dtype)


def _attention(dq, gq, dk, dv, gk, gv, lamv, subln_g, *, tq):
    b, s, _ = dq.shape
    sk = dk.shape[1]
    qspec = lambda w: pl.BlockSpec((1, tq, w), lambda bi, qi: (bi, qi, 0))
    kspec = lambda w: pl.BlockSpec((1, sk, w), lambda bi, qi: (bi, 0, 0))
    const2 = lambda bi, qi: (0, 0)
    mix_w = DV_W + GQ_W
    return pl.pallas_call(
        _attn_kernel,
        grid=(b, s // tq),
        in_specs=[qspec(DQ_W), qspec(GQ_W), kspec(DQ_W), kspec(DV_W), kspec(GK_W), kspec(GK_W),
                  pl.BlockSpec((4, HEAD_DIM), const2), pl.BlockSpec((1, 2 * HEAD_DIM), const2)],
        out_specs=pl.BlockSpec((1, tq, mix_w), lambda bi, qi: (bi, qi, 0)),
        out_shape=jax.ShapeDtypeStruct((b, s, mix_w), bf16),
        compiler_params=_cparams("parallel", "arbitrary"),
        name="attn",
    )(dq, gq, dk, dv, gk, gv, lamv, subln_g)


def _outproj_kernel(mix_ref, x_ref, mod_ref, wo_ref, gpost_ref, gpre_ref, wr_ref, x1_ref, h2_ref, st_ref):
    o = jnp.dot(mix_ref[0], wo_ref[...], preferred_element_type=f32)
    x1 = x_ref[0] + mod_ref[0, 2:3, :] * (_rms(o) * gpost_ref[...])
    x1_ref[0] = x1
    h2 = _rms(x1) * gpre_ref[...]
    h2 = (h2 * (1.0 + mod_ref[0, 4:5, :]) + mod_ref[0, 3:4, :]).astype(bf16)
    h2_ref[0] = h2
    st_ref[...] = jax.nn.sigmoid(lax.dot_general(wr_ref[...], h2, NT_DIMS, preferred_element_type=f32))


def _outproj(mixed, x, mod, w_out_b, g_post, g_pre, wr_t, *, tm):
    b, s, d = x.shape
    nt = s // tm
    ne = wr_t.shape[0]
    mw = mixed.shape[2]
    const2 = lambda bi, i: (0, 0)
    tile = lambda w: pl.BlockSpec((1, tm, w), lambda bi, i: (bi, i, 0))
    return pl.pallas_call(
        _outproj_kernel,
        grid=(b, nt),
        in_specs=[tile(mw), tile(d), pl.BlockSpec((1, N_MOD, d), lambda bi, i: (bi, 0, 0)),
                  pl.BlockSpec((mw, d), const2), pl.BlockSpec((1, d), const2), pl.BlockSpec((1, d), const2),
                  pl.BlockSpec((ne, d), const2)],
        out_specs=[tile(d), tile(d), pl.BlockSpec((ne, tm), lambda bi, i: (0, bi * nt + i))],
        out_shape=[jax.ShapeDtypeStruct((b, s, d), f32), jax.ShapeDtypeStruct((b, s, d), bf16),
                   jax.ShapeDtypeStruct((ne, b * s), f32)],
        compiler_params=_cparams("parallel", "arbitrary"),
        name="outproj",
    )(mixed, x, mod, w_out_b, g_post, g_pre, wr_t)


def _first_argmax(v, iota, n):
    m = jnp.max(v, axis=0, keepdims=True)
    idx = jnp.min(jnp.where(v == m, iota, n), axis=0, keepdims=True)
    return m, idx


def _route_gates_t(s, bias):
    ne, tm = s.shape
    gsz = ne // N_GROUPS
    neg = -jnp.inf
    ssel = s + bias
    sub = lax.broadcasted_iota(jnp.int32, (gsz, tm), 0).astype(f32)
    gscore = []
    for g in range(N_GROUPS):
        blk = ssel[gsz * g:gsz * (g + 1)]
        m1, i1 = _first_argmax(blk, sub, float(gsz))
        m2 = jnp.max(jnp.where(sub == i1, neg, blk), axis=0, keepdims=True)
        gscore.append(m1 + m2)
    gs = jnp.concatenate(gscore, axis=0)
    giota = lax.broadcasted_iota(jnp.int32, (N_GROUPS, tm), 0).astype(f32)
    gsel = jnp.zeros((N_GROUPS, tm), f32)
    for _ in range(TOPK_GROUPS):
        _, gi = _first_argmax(gs, giota, float(N_GROUPS))
        hit = giota == gi
        gsel = jnp.where(hit, 1.0, gsel)
        gs = jnp.where(hit, neg, gs)
    emask = jnp.concatenate([jnp.broadcast_to(gsel[g:g + 1], (gsz, tm)) for g in range(N_GROUPS)], axis=0)
    cand = jnp.where(emask > 0.0, ssel, neg)
    eiota = lax.broadcasted_iota(jnp.int32, (ne, tm), 0).astype(f32)
    w = jnp.zeros((ne, tm), f32)
    sel = jnp.zeros((ne, tm), f32)
    for _ in range(TOP_K):
        _, ei = _first_argmax(cand, eiota, float(ne))
        hit = eiota == ei
        w = jnp.where(hit, s, w)
        sel = jnp.where(hit, 1.0, sel)
        cand = jnp.where(hit, neg, cand)
    return w / jnp.sum(w, axis=0, keepdims=True) * ROUTED_SCALE, sel


def _route_kernel(st_ref, bias_ref, cnt_ref, posl_ref, tok_ref):
    gates, sel = _route_gates_t(st_ref[...], bias_ref[...])
    ne, sub = sel.shape
    n = jnp.sum(sel, axis=1, keepdims=True)
    cnt_ref[...] = jnp.broadcast_to(n, (ne, LANES))
    c = jnp.floor((n + (ROW_ALIGN - 1)) * (1.0 / ROW_ALIGN)) * ROW_ALIGN
    lower = (lax.broadcasted_iota(jnp.int32, (ne, ne), 1) < lax.broadcasted_iota(jnp.int32, (ne, ne), 0)).astype(f32)
    base = jnp.dot(lower, jnp.broadcast_to(c, (ne, LANES)), preferred_element_type=f32)[:, 0:1]
    upper = (lax.broadcasted_iota(jnp.int32, (sub, sub), 0) < lax.broadcasted_iota(jnp.int32, (sub, sub), 1)).astype(bf16)
    rank = jnp.dot(sel.astype(bf16), upper, preferred_element_type=f32)
    pos = base + rank
    eiota = lax.broadcasted_iota(jnp.int32, (ne, sub), 0).astype(f32)
    rem = sel
    pos_rows, w_rows = [], []
    for _ in range(TOP_K):
        ei = jnp.min(jnp.where(rem > 0.0, eiota, float(ne)), axis=0, keepdims=True)
        hit = eiota == ei
        pos_rows.append(jnp.sum(jnp.where(hit, pos, 0.0), axis=0, keepdims=True))
        w_rows.append(jnp.sum(jnp.where(hit, gates, 0.0), axis=0, keepdims=True))
        rem = jnp.where(hit, 0.0, rem)
    posl = jnp.concatenate(pos_rows, axis=0)
    posl_ref[0] = posl
    meta = jnp.concatenate([posl] + w_rows + [jnp.zeros((LANES - 2 * TOP_K, sub), f32)], axis=0)
    tok_ref[...] = meta.T


def _route(st, bias):
    ne, t = st.shape
    nt = t // SUB
    return pl.pallas_call(
        _route_kernel,
        grid=(nt,),
        in_specs=[pl.BlockSpec((ne, SUB), lambda i: (0, i)), pl.BlockSpec((ne, 1), lambda i: (0, 0))],
        out_specs=[pl.BlockSpec((ne, LANES), lambda i: (i, 0)),
                   pl.BlockSpec((1, TOP_K, SUB), lambda i: (i, 0, 0)),
                   pl.BlockSpec((SUB, LANES), lambda i: (i, 0))],
        out_shape=[jax.ShapeDtypeStruct((nt * ne, LANES), f32),
                   jax.ShapeDtypeStruct((nt, TOP_K, SUB), f32),
                   jax.ShapeDtypeStruct((t, LANES), f32)],
        compiler_params=_cparams("parallel"),
        name="route",
    )(st, bias)


def _piece_copy(loc_ref, hbm_ref, sem, slot, lo, go, to_hbm):
    src = loc_ref.at[slot, pl.ds(lo, ROW_ALIGN)]
    dst = hbm_ref.at[pl.ds(go, ROW_ALIGN)]
    return pltpu.make_async_copy(src, dst, sem.at[slot]) if to_hbm else pltpu.make_async_copy(dst, src, sem.at[slot])


def _start_pieces(meta_ref, loc_ref, hbm_ref, sem, slot, to_hbm):
    ne = meta_ref.shape[2]

    def per_expert(e, base):
        c = meta_ref[0, 0, e]
        g = meta_ref[0, 1, e]

        def per_piece(j, carry):
            lo = pl.multiple_of(base + j * ROW_ALIGN, ROW_ALIGN)
            go = pl.multiple_of(g + j * ROW_ALIGN, ROW_ALIGN)
            _piece_copy(loc_ref, hbm_ref, sem, slot, lo, go, to_hbm).start()
            return carry

        lax.fori_loop(0, c // ROW_ALIGN, per_piece, 0)
        return base + c

    return lax.fori_loop(0, ne, per_expert, 0) // ROW_ALIGN


def _wait_pieces(n, loc_ref, hbm_ref, sem, slot, to_hbm):
    def body(j, carry):
        _piece_copy(loc_ref, hbm_ref, sem, slot, 0, 0, to_hbm).wait()
        return carry

    lax.fori_loop(0, n, body, 0)


def _dispatch_kernel(meta_ref, posl_ref, h_ref, xs_ref, loc_ref, sem, npieces):
    i = pl.program_id(0)
    last = pl.num_programs(0) - 1
    slot = i % 2
    sub = h_ref.shape[0]

    @pl.when(i >= 2)
    def _():
        _wait_pieces(npieces[slot], loc_ref, xs_ref, sem, slot, True)

    def tile_rows(e, acc):
        return acc + meta_ref[0, 0, e]

    tot = lax.fori_loop(0, meta_ref.shape[2], tile_rows, 0)
    h = h_ref[...]
    posl = posl_ref[0]
    for r in range(NLOC // sub):
        @pl.when(r * sub < tot)
        def _():
            piota = (lax.broadcasted_iota(jnp.int32, (sub, sub), 0) + r * sub).astype(f32)
            onehot = jnp.zeros((sub, sub), f32)
            for k in range(TOP_K):
                onehot = jnp.where(piota == posl[k:k + 1], 1.0, onehot)
            rows = jnp.dot(onehot.astype(bf16), h, preferred_element_type=f32)
            loc_ref[slot, r * sub:(r + 1) * sub, :] = rows.astype(loc_ref.dtype)

    npieces[slot] = _start_pieces(meta_ref, loc_ref, xs_ref, sem, slot, True)

    @pl.when(i == last)
    def _():
        _wait_pieces(npieces[slot], loc_ref, xs_ref, sem, slot, True)

        @pl.when(i >= 1)
        def _():
            _wait_pieces(npieces[1 - slot], loc_ref, xs_ref, sem, 1 - slot, True)


def _dispatch(meta, posl, h2, rows_cap):
    t, d = h2.shape
    nt = t // SUB
    ne = meta.shape[2]
    return pl.pallas_call(
        _dispatch_kernel,
        grid=(nt,),
        in_specs=[pl.BlockSpec((1, 2, ne), lambda i: (i, 0, 0), memory_space=pltpu.SMEM),
                  pl.BlockSpec((1, TOP_K, SUB), lambda i: (i, 0, 0)),
                  pl.BlockSpec((SUB, d), lambda i: (i, 0))],
        out_specs=pl.BlockSpec(memory_space=pl.ANY),
        out_shape=jax.ShapeDtypeStruct((rows_cap, d), bf16),
        scratch_shapes=[pltpu.VMEM((2, NLOC, d), bf16), pltpu.SemaphoreType.DMA((2,)), pltpu.SMEM((2,), jnp.int32)],
        compiler_params=_cparams("arbitrary"),
        name="dispatch",
    )(meta, posl, h2)


def _gmm_kernel(be_ref, nv_ref, nb_ref, x_ref, wg_ref, wu_ref, wd_ref, y_ref):
    i = pl.program_id(0)
    nv = nv_ref[i]

    @pl.when(nv > 0)
    def _():
        x = x_ref[...]
        row = lax.broadcasted_iota(jnp.int32, x.shape, 0)
        x = jnp.where(row < nv, x, jnp.zeros((), x.dtype))
        a = _swiglu_hidden(x, wg_ref[0], wu_ref[0])
        y_ref[...] = jnp.dot(a.astype(bf16), wd_ref[0], preferred_element_type=f32).astype(y_ref.dtype)


def _gmm(be, nv, nb, xs, weg, weu, wed):
    rows_cap, d = xs.shape
    de = weg.shape[2]
    blk = lambda i, be, nv, nb: (jnp.minimum(i, nb[0] - 1), 0)
    wsel = lambda i, be, nv, nb: (be[i], 0, 0)
    return pl.pallas_call(
        _gmm_kernel,
        grid_spec=pltpu.PrefetchScalarGridSpec(
            num_scalar_prefetch=3,
            grid=(rows_cap // GMM_BLOCK,),
            in_specs=[pl.BlockSpec((GMM_BLOCK, d), blk),
                      pl.BlockSpec((1, d, de), wsel), pl.BlockSpec((1, d, de), wsel), pl.BlockSpec((1, de, d), wsel)],
            out_specs=pl.BlockSpec((GMM_BLOCK, d), blk)),
        out_shape=jax.ShapeDtypeStruct((rows_cap, d), bf16),
        compiler_params=_cparams("arbitrary"),
        name="gmm",
    )(be, nv, nb, xs, weg, weu, wed)


def _combine_kernel(meta_ref, metan_ref, tok_ref, h_ref, wsg_ref, wsu_ref, wsd_ref, x1_ref, mod_ref, gp_ref, ys_ref,
                    o_ref, loc_ref, sem, npieces):
    i = pl.program_id(0)
    last = pl.num_programs(0) - 1
    slot = i % 2
    sub = h_ref.shape[0]

    @pl.when(i == 0)
    def _():
        loc_ref[...] = jnp.zeros(loc_ref.shape, loc_ref.dtype)
        npieces[0] = _start_pieces(meta_ref, loc_ref, ys_ref, sem, 0, False)

    @pl.when(i < last)
    def _():
        npieces[1 - slot] = _start_pieces(metan_ref, loc_ref, ys_ref, sem, 1 - slot, False)

    h = h_ref[...]
    a = _swiglu_hidden(h, wsg_ref[...], wsu_ref[...])
    y = jnp.dot(a.astype(bf16), wsd_ref[...], preferred_element_type=f32)

    tot = npieces[slot] * ROW_ALIGN
    _wait_pieces(npieces[slot], loc_ref, ys_ref, sem, slot, False)
    tok = tok_ref[...]
    acc_ref = o_ref
    acc_ref[...] = y
    for r in range(NLOC // sub):
        @pl.when(r * sub < tot)
        def _():
            liota = (lax.broadcasted_iota(jnp.int32, (sub, sub), 1) + r * sub).astype(f32)
            wmat = jnp.zeros((sub, sub), f32)
            for k in range(TOP_K):
                wmat = jnp.where(liota == tok[:, k:k + 1], tok[:, TOP_K + k:TOP_K + k + 1], wmat)
            acc_ref[...] += jnp.dot(wmat.astype(bf16), loc_ref[slot, r * sub:(r + 1) * sub, :],
                                    preferred_element_type=f32)

    o_ref[...] = x1_ref[...] + mod_ref[0, 5:6, :] * (_rms(acc_ref[...]) * gp_ref[...])


def _combine(meta, tok, h2, wsg, wsu, wsd, x1, mod, g_post, ys, *, s):
    t, d = h2.shape
    nt = t // SUB
    ne = meta.shape[2]
    ds = wsg.shape[1]
    per_b = s // SUB
    tile = lambda w: pl.BlockSpec((SUB, w), lambda i: (i, 0))
    const2 = lambda i: (0, 0)
    return pl.pallas_call(
        _combine_kernel,
        grid=(nt,),
        in_specs=[pl.BlockSpec((1, 2, ne), lambda i: (i, 0, 0), memory_space=pltpu.SMEM),
                  pl.BlockSpec((1, 2, ne), lambda i: (jnp.minimum(i + 1, nt - 1), 0, 0), memory_space=pltpu.SMEM),
                  tile(LANES), tile(d),
                  pl.BlockSpec((d, ds), const2), pl.BlockSpec((d, ds), const2), pl.BlockSpec((ds, d), const2),
                  tile(d), pl.BlockSpec((1, N_MOD, d), lambda i: (i // per_b, 0, 0)), pl.BlockSpec((1, d), const2),
                  pl.BlockSpec(memory_space=pl.ANY)],
        out_specs=tile(d),
        out_shape=jax.ShapeDtypeStruct((t, d), f32),
        scratch_shapes=[pltpu.VMEM((2, NLOC, d), bf16), pltpu.SemaphoreType.DMA((2,)), pltpu.SMEM((2,), jnp.int32)],
        compiler_params=_cparams("arbitrary"),
        name="combine",
    )(meta, meta, tok, h2, wsg, wsu, wsd, x1, mod, g_post, ys)


def _tiles(s, sc):
    tm = math.gcd(s, 512)
    assert sc % tm == 0 or tm % sc == 0
    return dict(tm=tm, tm_ctx=math.gcd(sc, tm), tq=math.gcd(s, 256))


def _dispatch_tables(cnt, nt, ne, rows_cap):
    n = cnt.reshape(nt, ne, LANES)[:, :, 0].astype(jnp.int32)
    c = (n + (ROW_ALIGN - 1)) // ROW_ALIGN * ROW_ALIGN
    n_e = jnp.sum(c, axis=0)
    p_e = (n_e + (GMM_BLOCK - 1)) // GMM_BLOCK * GMM_BLOCK
    seg_end = jnp.cumsum(p_e)
    seg_start = seg_end - p_e
    g = seg_start[None, :] + jnp.cumsum(c, axis=0) - c
    meta = jnp.stack([c, g], axis=1)
    blk_start = jnp.arange(rows_cap // GMM_BLOCK, dtype=jnp.int32) * GMM_BLOCK
    be = jnp.minimum(jnp.searchsorted(seg_end, blk_start, side="right"), ne - 1).astype(jnp.int32)
    nv = jnp.clip(n_e[be] - (blk_start - seg_start[be]), 0, GMM_BLOCK).astype(jnp.int32)
    nb = jnp.maximum(seg_end[-1:] // GMM_BLOCK, 1).astype(jnp.int32)
    return meta, be, nv, nb


def _rope_tables(s):
    n_freq = HEAD_DIM // 4
    inv_freq = ROPE_THETA ** (-jnp.arange(n_freq, dtype=f32) / n_freq)
    pos = jnp.arange(s)
    ang_r = (pos // GRID_W).astype(f32)[:, None] * inv_freq
    ang_c = (pos % GRID_W).astype(f32)[:, None] * inv_freq
    cr, sr, cc, sc_ = jnp.cos(ang_r), jnp.sin(ang_r), jnp.cos(ang_c), jnp.sin(ang_c)
    cos = jnp.concatenate([cr, cr, cc, cc] * (LANES // HEAD_DIM), axis=1)
    sin = jnp.concatenate([-sr, sr, -sc_, sc_] * (LANES // HEAD_DIM), axis=1)
    return cos, sin


def kernel(x, c, ctx, c_ctx, w_mod, b_mod, g_pre_mix, g_post_mix, w_in, q_norm_g, k_norm_g, lambda_q1, lambda_k1,
           lambda_q2, lambda_k2, subln_g, w_out, g_pre_ffn, g_post_ffn, w_router, router_bias, w_exp_gate, w_exp_up,
           w_exp_down, w_sh_gate, w_sh_up, w_sh_down):
    assert w_mod.shape[0] == 1, "single-layer block"
    b, s, d = x.shape
    sc = ctx.shape[1]
    sk = s + sc
    ne = w_router.shape[2]
    assert w_in.shape[2] == IN_WIDTH and ne % N_GROUPS == 0 and ne <= LANES
    tl = _tiles(s, sc)

    pad = (-(b + 1)) % 8
    cc = jnp.concatenate([c, c_ctx[None], jnp.zeros((pad, d), f32)], axis=0)
    mod_all = _mod(cc, w_mod[0], b_mod[0])
    mod = mod_all[:b].reshape(b, N_MOD, d)
    mod_c = mod_all[b:b + 1].reshape(1, N_MOD, d)

    w_in_b = w_in[0].astype(bf16)
    qg = jnp.tile(q_norm_g[0], GQA_HEADS)[None]
    kg = jnp.tile(k_norm_g[0], GQA_KV_HEADS)[None]
    head_id = jnp.arange(GQ_W) // HEAD_DIM
    bd = (head_id[:, None] == head_id[None, :]).astype(bf16)
    cos, sin = _rope_tables(s)
    g_pre = g_pre_mix[0][None]

    dq, gq, dk, dv, gk, gv = _inproj(x, mod, g_pre, w_in_b, qg, kg, bd, cos, sin, None,
                                     latent=True, sk=sk, row_off=0, tm=tl["tm"])
    dk, dv, gk, gv = _inproj(ctx, mod_c, g_pre, w_in_b, qg, kg, bd, cos, sin, (dk, dv, gk, gv),
                             latent=False, sk=sk, row_off=s, tm=tl["tm_ctx"])

    lamv = jnp.stack([lambda_q1[0], lambda_k1[0], lambda_q2[0], lambda_k2[0]])
    mixed = _attention(dq, gq, dk, dv, gk, gv, lamv, subln_g[0][None], tq=tl["tq"])

    x1, h2, st = _outproj(mixed, x, mod, w_out[0].astype(bf16), g_post_mix[0][None], g_pre_ffn[0][None],
                          w_router[0].T.astype(bf16), tm=tl["tm"])
    t = b * s
    nt = t // SUB
    assert s % SUB == 0 and ne <= N_EXPERTS_MAX
    cnt, posl, tok = _route(st, router_bias[0][:, None])
    rows_cap = -(-(t * TOP_K + nt * ne * (ROW_ALIGN - 1) + ne * (GMM_BLOCK - 1)) // GMM_BLOCK) * GMM_BLOCK
    meta, be, nv, nb = _dispatch_tables(cnt, nt, ne, rows_cap)
    h2f = h2.reshape(t, d)
    xs = _dispatch(meta, posl, h2f, rows_cap)
    ys = _gmm(be, nv, nb, xs, w_exp_gate[0].astype(bf16), w_exp_up[0].astype(bf16), w_exp_down[0].astype(bf16))
    out = _combine(meta, tok, h2f, w_sh_gate[0].astype(bf16), w_sh_up[0].astype(bf16), w_sh_down[0].astype(bf16),
                   x1.reshape(t, d), mod, g_post_ffn[0][None], ys, s=s)
    return out.reshape(b, s, d)
```

```python
import functools
import math

import jax
import jax.numpy as jnp
from jax import lax
from jax.experimental import pallas as pl
from jax.experimental.pallas import tpu as pltpu

HEAD_DIM = 64
DIFF_HEADS = 4
GQA_HEADS = 8
GQA_KV_HEADS = 2
GRID_W = 64
ROPE_THETA = 10000.0
N_GROUPS = 8
TOPK_GROUPS = 4
TOP_K = 8
ROUTED_SCALE = 2.5
N_MOD = 6
EPS = 1e-6
LAM_INIT = 0.8 - 0.6 * math.exp(-0.3 * 0)

LANES = 128
DQ_W = DIFF_HEADS * 2 * HEAD_DIM
DV_W = DIFF_HEADS * 2 * HEAD_DIM
GQ_W = GQA_HEADS * HEAD_DIM
GK_W = GQA_KV_HEADS * HEAD_DIM
OFF_DQ, OFF_DK, OFF_DV = 0, DQ_W, 2 * DQ_W
OFF_GQ = OFF_DV + DV_W
OFF_GK = OFF_GQ + GQ_W
OFF_GV = OFF_GK + GK_W
IN_WIDTH = OFF_GV + GK_W

VMEM_LIMIT = 56 * 1024 * 1024

N_EXPERTS_MAX = 64
SUB = 256
ROW_ALIGN = 16
GMM_BLOCK = 512
NLOC = -(-(SUB * TOP_K + N_EXPERTS_MAX * (ROW_ALIGN - 1)) // SUB) * SUB


def _swiglu_hidden(h, wg, wu):
    return jax.nn.silu(jnp.dot(h, wg, preferred_element_type=f32)) * jnp.dot(h, wu, preferred_element_type=f32)

f32 = jnp.float32
bf16 = jnp.bfloat16
NT_DIMS = (((1,), (1,)), ((), ()))
TN_DIMS = (((0,), (0,)), ((), ()))


def _cparams(*sem):
    return pltpu.CompilerParams(dimension_semantics=sem, vmem_limit_bytes=VMEM_LIMIT)


def _rms(x):
    return x * lax.rsqrt(jnp.mean(x * x, axis=-1, keepdims=True) + EPS)


def _mod_kernel(c_ref, w_ref, b_ref, o_ref):
    o_ref[...] = jnp.dot(jax.nn.silu(c_ref[...]), w_ref[...], preferred_element_type=f32) + b_ref[...]


def _mod(cc, w_mod, b_mod):
    rows, d = cc.shape
    n = w_mod.shape[1]
    bn = d
    return pl.pallas_call(
        _mod_kernel,
        grid=(n // bn,),
        in_specs=[pl.BlockSpec((rows, d), lambda j: (0, 0)),
                  pl.BlockSpec((d, bn), lambda j: (0, j)),
                  pl.BlockSpec((1, bn), lambda j: (0, j))],
        out_specs=pl.BlockSpec((rows, bn), lambda j: (0, j)),
        out_shape=jax.ShapeDtypeStruct((rows, n), f32),
        compiler_params=_cparams("arbitrary"),
        name="mod",
    )(cc, w_mod, b_mod.reshape(1, n))


def _rope_tile(xt, cos, sin_signed, lane_lo):
    partner = jnp.where(lane_lo, pltpu.roll(xt, LANES - 16, 1), pltpu.roll(xt, 16, 1))
    return xt * cos + partner * sin_signed


def _inproj_kernel(x_ref, mod_ref, g_ref, w_ref, qg_ref, kg_ref, bd_ref, cos_ref, sin_ref, *out_refs,
                   latent):
    x = x_ref[0]
    h = _rms(x) * g_ref[...]
    h = h * (1.0 + mod_ref[0, 1:2, :]) + mod_ref[0, 0:1, :]
    hb = h.astype(bf16)
    tm = x.shape[0]

    def proj(off, width):
        return jnp.dot(hb, w_ref[:, off:off + width], preferred_element_type=f32)

    def head_rms(v, gain):
        w = v.shape[1]
        ss = jnp.dot((v * v).astype(bf16), bd_ref[0:w, 0:w], preferred_element_type=f32)
        return v * lax.rsqrt(ss * (1.0 / HEAD_DIM) + EPS) * gain

    if latent:
        dq_ref, gq_ref, dk_ref, dv_ref, gk_ref, gv_ref = out_refs
        cos, sin = cos_ref[...], sin_ref[...]
        lane_lo = (lax.broadcasted_iota(jnp.int32, (tm, LANES), 1) % 32) < 16
        rope = lambda v: _rope_tile(v, cos, sin, lane_lo)
    else:
        dk_ref, dv_ref, gk_ref, gv_ref = out_refs[-4:]
        rope = lambda v: v

    def store(ref, v, fn):
        for t in range(v.shape[1] // LANES):
            sl = slice(LANES * t, LANES * (t + 1))
            ref[0, :, sl] = fn(v[:, sl]).astype(ref.dtype)

    qscale = HEAD_DIM ** -0.5 * math.log2(math.e)
    if latent:
        store(dq_ref, proj(OFF_DQ, DQ_W) * qscale, rope)
        store(gq_ref, head_rms(proj(OFF_GQ, GQ_W), qg_ref[...] * qscale), rope)
    store(dk_ref, proj(OFF_DK, DQ_W), rope)
    store(gk_ref, head_rms(proj(OFF_GK, GK_W), kg_ref[...]), rope)
    dv_ref[0] = proj(OFF_DV, DV_W).astype(bf16)
    gv_ref[0] = proj(OFF_GV, GK_W).astype(bf16)


def _inproj(x, mod, g_pre, w_in_b, qg, kg, bd, cos, sin, kv_bufs, *, latent, sk, row_off, tm):
    b, s, d = x.shape
    nt = s // tm
    kv_blk = row_off // tm
    kv_widths = (DQ_W, DV_W, GK_W, GK_W)
    const2 = lambda bi, i: (0, 0)
    in_specs = [
        pl.BlockSpec((1, tm, d), lambda bi, i: (bi, i, 0)),
        pl.BlockSpec((1, N_MOD, d), (lambda bi, i: (bi, 0, 0)) if latent else (lambda bi, i: (0, 0, 0))),
        pl.BlockSpec((1, d), const2),
        pl.BlockSpec((d, IN_WIDTH), const2),
        pl.BlockSpec((1, GQ_W), const2),
        pl.BlockSpec((1, GK_W), const2),
        pl.BlockSpec((GQ_W, GQ_W), const2),
        pl.BlockSpec((tm, LANES), (lambda bi, i: (i, 0)) if latent else const2),
        pl.BlockSpec((tm, LANES), (lambda bi, i: (i, 0)) if latent else const2),
    ]
    kv_specs = [pl.BlockSpec((1, tm, w), lambda bi, i: (bi, kv_blk + i, 0)) for w in kv_widths]
    kv_shapes = [jax.ShapeDtypeStruct((b, sk, w), bf16) for w in kv_widths]
    args = [x, mod, g_pre, w_in_b, qg, kg, bd, cos, sin]
    if latent:
        out_specs = [pl.BlockSpec((1, tm, DQ_W), lambda bi, i: (bi, i, 0)),
                     pl.BlockSpec((1, tm, GQ_W), lambda bi, i: (bi, i, 0))] + kv_specs
        out_shape = [jax.ShapeDtypeStruct((b, s, DQ_W), bf16), jax.ShapeDtypeStruct((b, s, GQ_W), bf16)] + kv_shapes
        aliases = {}
    else:
        in_specs += [pl.BlockSpec(memory_space=pl.ANY)] * 4
        args += list(kv_bufs)
        out_specs, out_shape = kv_specs, kv_shapes
        aliases = {len(args) - 4 + k: k for k in range(4)}
    return pl.pallas_call(
        functools.partial(_inproj_kernel, latent=latent),
        grid=(b, nt),
        in_specs=in_specs,
        out_specs=out_specs,
        out_shape=out_shape,
        input_output_aliases=aliases,
        compiler_params=_cparams("parallel", "arbitrary"),
        name="inproj_latent" if latent else "inproj_ctx",
    )(*args)


def _softmax_pv(q, k, vext):
    s = lax.dot_general(q, k, NT_DIMS, preferred_element_type=f32)
    p = jnp.exp2(s - jnp.max(s, axis=-1, keepdims=True)).astype(bf16)
    r = jnp.dot(p, vext, preferred_element_type=f32)
    return r[:, :LANES] / r[:, LANES:]


def _attn_kernel(dq_ref, gq_ref, dk_ref, dv_ref, gk_ref, gv_ref, lam_ref, sg_ref, o_ref, vext_ref):
    tq = dq_ref.shape[1]
    sk = dk_ref.shape[1]

    @pl.when(pl.program_id(1) == 0)
    def _():
        ones = jnp.ones((sk, LANES), bf16)
        for h in range(DIFF_HEADS):
            vext_ref[h, :, :LANES] = dv_ref[0, :, LANES * h:LANES * (h + 1)]
            vext_ref[h, :, LANES:] = ones
        vext_ref[DIFF_HEADS, :, :LANES] = gv_ref[0]
        vext_ref[DIFF_HEADS, :, LANES:] = ones

    lane = lax.broadcasted_iota(jnp.int32, (tq, LANES), 1)
    lo = lane < HEAD_DIM
    lv = lam_ref[...]
    lam = (jnp.exp(jnp.sum(lv[0:1] * lv[1:2], axis=1, keepdims=True))
           - jnp.exp(jnp.sum(lv[2:3] * lv[3:4], axis=1, keepdims=True)) + LAM_INIT)
    zero = jnp.zeros((), bf16)

    for h in range(DIFF_HEADS):
        sl = slice(LANES * h, LANES * (h + 1))
        qt = dq_ref[0, :, sl]
        kt = dk_ref[0, :, sl]
        o1 = _softmax_pv(jnp.where(lo, qt, zero), kt, vext_ref[h])
        o2 = _softmax_pv(jnp.where(lo, zero, qt), kt, vext_ref[h])
        o = _rms(o1 - lam * o2) * (sg_ref[...] * (1.0 - LAM_INIT))
        o_ref[0, :, sl] = o.astype(o_ref.dtype)

    group = GQA_HEADS // GQA_KV_HEADS
    kt = gk_ref[0]
    for t in range(GQA_HEADS // 2):
        halves = []
        for pos in range(2):
            hd = 2 * t + pos
            j = hd // group
            qt = gq_ref[0, :, LANES * t:LANES * (t + 1)]
            if pos != j:
                qt = pltpu.roll(qt, HEAD_DIM, 1)
            o = _softmax_pv(jnp.where(lo if j == 0 else ~lo, qt, zero), kt, vext_ref[DIFF_HEADS])
            if pos != j:
                o = pltpu.roll(o, HEAD_DIM, 1)
            halves.append(o)
        off = DIFF_HEADS * 2 * HEAD_DIM + LANES * t
        o_ref[0, :, off:off + LANES] = jnp.where(lo, halves[0], halves[1]).astype(o_ref.dtype)


def _attention(dq, gq, dk, dv, gk, gv, lamv, subln_g, *, tq):
    b, s, _ = dq.shape
    sk = dk.shape[1]
    qspec = lambda w: pl.BlockSpec((1, tq, w), lambda bi, qi: (bi, qi, 0))
    kspec = lambda w: pl.BlockSpec((1, sk, w), lambda bi, qi: (bi, 0, 0))
    const2 = lambda bi, qi: (0, 0)
    mix_w = DV_W + GQ_W
    return pl.pallas_call(
        _attn_kernel,
        grid=(b, s // tq),
        in_specs=[qspec(DQ_W), qspec(GQ_W), kspec(DQ_W), kspec(DV_W), kspec(GK_W), kspec(GK_W),
                  pl.BlockSpec((4, HEAD_DIM), const2), pl.BlockSpec((1, 2 * HEAD_DIM), const2)],
        out_specs=pl.BlockSpec((1, tq, mix_w), lambda bi, qi: (bi, qi, 0)),
        out_shape=jax.ShapeDtypeStruct((b, s, mix_w), bf16),
        scratch_shapes=[pltpu.VMEM((DIFF_HEADS + 1, sk, 2 * LANES), bf16)],
        compiler_params=_cparams("parallel", "arbitrary"),
        name="attn",
    )(dq, gq, dk, dv, gk, gv, lamv, subln_g)


def _outproj_kernel(mix_ref, x_ref, mod_ref, wo_ref, gpost_ref, gpre_ref, wr_ref, x1_ref, h2_ref, st_ref):
    o = jnp.dot(mix_ref[0], wo_ref[...], preferred_element_type=f32)
    x1 = x_ref[0] + mod_ref[0, 2:3, :] * (_rms(o) * gpost_ref[...])
    x1_ref[0] = x1
    h2 = _rms(x1) * gpre_ref[...]
    h2 = (h2 * (1.0 + mod_ref[0, 4:5, :]) + mod_ref[0, 3:4, :]).astype(bf16)
    h2_ref[0] = h2
    st_ref[...] = jax.nn.sigmoid(lax.dot_general(wr_ref[...], h2, NT_DIMS, preferred_element_type=f32))


def _outproj(mixed, x, mod, w_out_b, g_post, g_pre, wr_t, *, tm):
    b, s, d = x.shape
    nt = s // tm
    ne = wr_t.shape[0]
    mw = mixed.shape[2]
    const2 = lambda bi, i: (0, 0)
    tile = lambda w: pl.BlockSpec((1, tm, w), lambda bi, i: (bi, i, 0))
    return pl.pallas_call(
        _outproj_kernel,
        grid=(b, nt),
        in_specs=[tile(mw), tile(d), pl.BlockSpec((1, N_MOD, d), lambda bi, i: (bi, 0, 0)),
                  pl.BlockSpec((mw, d), const2), pl.BlockSpec((1, d), const2), pl.BlockSpec((1, d), const2),
                  pl.BlockSpec((ne, d), const2)],
        out_specs=[tile(d), tile(d), pl.BlockSpec((ne, tm), lambda bi, i: (0, bi * nt + i))],
        out_shape=[jax.ShapeDtypeStruct((b, s, d), f32), jax.ShapeDtypeStruct((b, s, d), bf16),
                   jax.ShapeDtypeStruct((ne, b * s), f32)],
        compiler_params=_cparams("parallel", "arbitrary"),
        name="outproj",
    )(mixed, x, mod, w_out_b, g_post, g_pre, wr_t)


def _first_argmax(v, iota, n):
    m = jnp.max(v, axis=0, keepdims=True)
    idx = jnp.min(jnp.where(v == m, iota, n), axis=0, keepdims=True)
    return m, idx


def _route_gates_t(s, bias):
    ne, tm = s.shape
    gsz = ne // N_GROUPS
    neg = -jnp.inf
    ssel = s + bias
    sub = lax.broadcasted_iota(jnp.int32, (gsz, tm), 0).astype(f32)
    gscore = []
    for g in range(N_GROUPS):
        blk = ssel[gsz * g:gsz * (g + 1)]
        m1, i1 = _first_argmax(blk, sub, float(gsz))
        m2 = jnp.max(jnp.where(sub == i1, neg, blk), axis=0, keepdims=True)
        gscore.append(m1 + m2)
    gs = jnp.concatenate(gscore, axis=0)
    giota = lax.broadcasted_iota(jnp.int32, (N_GROUPS, tm), 0).astype(f32)
    gsel = jnp.zeros((N_GROUPS, tm), f32)
    for _ in range(TOPK_GROUPS):
        _, gi = _first_argmax(gs, giota, float(N_GROUPS))
        hit = giota == gi
        gsel = jnp.where(hit, 1.0, gsel)
        gs = jnp.where(hit, neg, gs)
    emask = jnp.concatenate([jnp.broadcast_to(gsel[g:g + 1], (gsz, tm)) for g in range(N_GROUPS)], axis=0)
    cand = jnp.where(emask > 0.0, ssel, neg)
    eiota = lax.broadcasted_iota(jnp.int32, (ne, tm), 0).astype(f32)
    w = jnp.zeros((ne, tm), f32)
    sel = jnp.zeros((ne, tm), f32)
    for _ in range(TOP_K):
        _, ei = _first_argmax(cand, eiota, float(ne))
        hit = eiota == ei
        w = jnp.where(hit, s, w)
        sel = jnp.where(hit, 1.0, sel)
        cand = jnp.where(hit, neg, cand)
    return w / jnp.sum(w, axis=0, keepdims=True) * ROUTED_SCALE, sel


def _route_kernel(st_ref, bias_ref, cnt_ref, posl_ref, gate_ref):
    gates, sel = _route_gates_t(st_ref[...], bias_ref[...])
    ne, sub = sel.shape
    n = jnp.sum(sel, axis=1, keepdims=True)
    cnt_ref[...] = jnp.broadcast_to(n, (ne, LANES))
    c = jnp.floor((n + (ROW_ALIGN - 1)) * (1.0 / ROW_ALIGN)) * ROW_ALIGN
    lower = (lax.broadcasted_iota(jnp.int32, (ne, ne), 1) < lax.broadcasted_iota(jnp.int32, (ne, ne), 0)).astype(f32)
    base = jnp.dot(lower, jnp.broadcast_to(c, (ne, LANES)), preferred_element_type=f32)[:, 0:1]
    upper = (lax.broadcasted_iota(jnp.int32, (sub, sub), 0) < lax.broadcasted_iota(jnp.int32, (sub, sub), 1)).astype(bf16)
    rank = jnp.dot(sel.astype(bf16), upper, preferred_element_type=f32)
    pos = base + rank
    eiota = lax.broadcasted_iota(jnp.int32, (ne, sub), 0).astype(f32)
    rem = sel
    pos_rows, w_rows = [], []
    for _ in range(TOP_K):
        ei = jnp.min(jnp.where(rem > 0.0, eiota, float(ne)), axis=0, keepdims=True)
        hit = eiota == ei
        pos_rows.append(jnp.sum(jnp.where(hit, pos, 0.0), axis=0, keepdims=True))
        w_rows.append(jnp.sum(jnp.where(hit, gates, 0.0), axis=0, keepdims=True))
        rem = jnp.where(hit, 0.0, rem)
    posl_ref[0] = jnp.concatenate(pos_rows, axis=0)
    gate_ref[0] = jnp.concatenate(w_rows, axis=0)


def _route(st, bias):
    ne, t = st.shape
    nt = t // SUB
    return pl.pallas_call(
        _route_kernel,
        grid=(nt,),
        in_specs=[pl.BlockSpec((ne, SUB), lambda i: (0, i)), pl.BlockSpec((ne, 1), lambda i: (0, 0))],
        out_specs=[pl.BlockSpec((ne, LANES), lambda i: (i, 0)),
                   pl.BlockSpec((1, TOP_K, SUB), lambda i: (i, 0, 0)),
                   pl.BlockSpec((1, TOP_K, SUB), lambda i: (i, 0, 0))],
        out_shape=[jax.ShapeDtypeStruct((nt * ne, LANES), f32),
                   jax.ShapeDtypeStruct((nt, TOP_K, SUB), f32),
                   jax.ShapeDtypeStruct((nt, TOP_K, SUB), f32)],
        compiler_params=_cparams("parallel"),
        name="route",
    )(st, bias)


def _rows_copy(loc_ref, hbm_ref, sem, slot, lo, go, rows, to_hbm):
    src = loc_ref.at[slot, pl.ds(lo, rows)]
    dst = hbm_ref.at[pl.ds(go, rows)]
    return pltpu.make_async_copy(src, dst, sem.at[slot]) if to_hbm else pltpu.make_async_copy(dst, src, sem.at[slot])


def _start_chunks(meta_ref, loc_ref, hbm_ref, sem, slot, to_hbm):
    def per_expert(e, base):
        c = pl.multiple_of(meta_ref[0, 0, e], ROW_ALIGN)
        g = pl.multiple_of(meta_ref[0, 1, e], ROW_ALIGN)

        @pl.when(c > 0)
        def _():
            _rows_copy(loc_ref, hbm_ref, sem, slot, pl.multiple_of(base, ROW_ALIGN), g, c, to_hbm).start()

        return base + c

    return lax.fori_loop(0, meta_ref.shape[2], per_expert, 0)


def _wait_rows(rows, loc_ref, hbm_ref, sem, slot, to_hbm):
    units = rows // ROW_ALIGN
    for bit in range((NLOC // ROW_ALIGN).bit_length()):
        @pl.when((units >> bit) & 1 == 1)
        def _():
            _rows_copy(loc_ref, hbm_ref, sem, slot, 0, 0, ROW_ALIGN << bit, to_hbm).wait()


def _select_chunk(posl, vals, r, sub):
    rel = posl - float(r * sub)
    rel = jnp.where((rel >= 0.0) & (rel < float(sub)), rel, -1.0).astype(bf16)
    piota = lax.broadcasted_iota(jnp.int32, (sub, sub), 0).astype(f32).astype(bf16)
    m = jnp.zeros((sub, sub), bf16)
    for k in range(TOP_K):
        val = jnp.ones((), bf16) if vals is None else vals[k:k + 1]
        m = jnp.where(piota == rel[k:k + 1], val, m)
    return m


def _dispatch_kernel(meta_ref, posl_ref, h_ref, xs_ref, loc_ref, sem, nrows):
    i = pl.program_id(0)
    last = pl.num_programs(0) - 1
    slot = i % 2
    sub = h_ref.shape[0]

    @pl.when(i >= 2)
    def _():
        _wait_rows(nrows[slot], loc_ref, xs_ref, sem, slot, True)

    def tile_rows(e, acc):
        return acc + meta_ref[0, 0, e]

    tot = lax.fori_loop(0, meta_ref.shape[2], tile_rows, 0)
    h = h_ref[...]
    posl = posl_ref[0]
    for r in range(NLOC // sub):
        @pl.when(r * sub < tot)
        def _():
            rows = jnp.dot(_select_chunk(posl, None, r, sub), h, preferred_element_type=f32)
            loc_ref[slot, r * sub:(r + 1) * sub, :] = rows.astype(loc_ref.dtype)

    nrows[slot] = _start_chunks(meta_ref, loc_ref, xs_ref, sem, slot, True)

    @pl.when(i == last)
    def _():
        _wait_rows(nrows[slot], loc_ref, xs_ref, sem, slot, True)

        @pl.when(i >= 1)
        def _():
            _wait_rows(nrows[1 - slot], loc_ref, xs_ref, sem, 1 - slot, True)


def _dispatch(meta, posl, h2, rows_cap):
    t, d = h2.shape
    nt = t // SUB
    ne = meta.shape[2]
    return pl.pallas_call(
        _dispatch_kernel,
        grid=(nt,),
        in_specs=[pl.BlockSpec((1, 2, ne), lambda i: (i, 0, 0), memory_space=pltpu.SMEM),
                  pl.BlockSpec((1, TOP_K, SUB), lambda i: (i, 0, 0)),
                  pl.BlockSpec((SUB, d), lambda i: (i, 0))],
        out_specs=pl.BlockSpec(memory_space=pl.ANY),
        out_shape=jax.ShapeDtypeStruct((rows_cap, d), bf16),
        scratch_shapes=[pltpu.VMEM((2, NLOC, d), bf16), pltpu.SemaphoreType.DMA((2,)), pltpu.SMEM((2,), jnp.int32)],
        compiler_params=_cparams("arbitrary"),
        name="dispatch",
    )(meta, posl, h2)


def _gmm_kernel(be_ref, nv_ref, nb_ref, x_ref, wg_ref, wu_ref, wd_ref, y_ref):
    i = pl.program_id(0)
    nv = nv_ref[i]

    @pl.when(nv > 0)
    def _():
        x = x_ref[...]
        row = lax.broadcasted_iota(jnp.int32, x.shape, 0)
        x = jnp.where(row < nv, x, jnp.zeros((), x.dtype))
        a = _swiglu_hidden(x, wg_ref[0], wu_ref[0])
        y_ref[...] = jnp.dot(a.astype(bf16), wd_ref[0], preferred_element_type=f32).astype(y_ref.dtype)


def _gmm(be, nv, nb, xs, weg, weu, wed):
    rows_cap, d = xs.shape
    de = weg.shape[2]
    blk = lambda i, be, nv, nb: (jnp.minimum(i, nb[0] - 1), 0)
    wsel = lambda i, be, nv, nb: (be[i], 0, 0)
    return pl.pallas_call(
        _gmm_kernel,
        grid_spec=pltpu.PrefetchScalarGridSpec(
            num_scalar_prefetch=3,
            grid=(rows_cap // GMM_BLOCK,),
            in_specs=[pl.BlockSpec((GMM_BLOCK, d), blk),
                      pl.BlockSpec((1, d, de), wsel), pl.BlockSpec((1, d, de), wsel), pl.BlockSpec((1, de, d), wsel)],
            out_specs=pl.BlockSpec((GMM_BLOCK, d), blk)),
        out_shape=jax.ShapeDtypeStruct((rows_cap, d), bf16),
        compiler_params=_cparams("arbitrary"),
        name="gmm",
    )(be, nv, nb, xs, weg, weu, wed)


def _combine_kernel(meta_ref, metan_ref, posl_ref, gate_ref, h_ref, wsg_ref, wsu_ref, wsd_ref, x1_ref, mod_ref, gp_ref,
                    ys_ref, o_ref, loc_ref, w_ref, sem, nrows):
    i = pl.program_id(0)
    last = pl.num_programs(0) - 1
    slot = i % 2
    sub = h_ref.shape[0]

    @pl.when(i == 0)
    def _():
        loc_ref[...] = jnp.zeros(loc_ref.shape, loc_ref.dtype)
        nrows[0] = _start_chunks(meta_ref, loc_ref, ys_ref, sem, 0, False)

    @pl.when(i < last)
    def _():
        nrows[1 - slot] = _start_chunks(metan_ref, loc_ref, ys_ref, sem, 1 - slot, False)

    h = h_ref[...]
    a = _swiglu_hidden(h, wsg_ref[...], wsu_ref[...])
    y = jnp.dot(a.astype(bf16), wsd_ref[...], preferred_element_type=f32)

    posl = posl_ref[0]
    gates = gate_ref[0].astype(bf16)
    for r in range(NLOC // sub):
        w_ref[r * sub:(r + 1) * sub, :] = _select_chunk(posl, gates, r, sub)
    _wait_rows(nrows[slot], loc_ref, ys_ref, sem, slot, False)
    y = y + lax.dot_general(w_ref[...], loc_ref[slot], TN_DIMS, preferred_element_type=f32)
    o_ref[...] = x1_ref[...] + mod_ref[0, 5:6, :] * (_rms(y) * gp_ref[...])


def _combine(meta, posl, gatel, h2, wsg, wsu, wsd, x1, mod, g_post, ys, *, s):
    t, d = h2.shape
    nt = t // SUB
    ne = meta.shape[2]
    ds = wsg.shape[1]
    per_b = s // SUB
    tile = lambda w: pl.BlockSpec((SUB, w), lambda i: (i, 0))
    const2 = lambda i: (0, 0)
    return pl.pallas_call(
        _combine_kernel,
        grid=(nt,),
        in_specs=[pl.BlockSpec((1, 2, ne), lambda i: (i, 0, 0), memory_space=pltpu.SMEM),
                  pl.BlockSpec((1, 2, ne), lambda i: (jnp.minimum(i + 1, nt - 1), 0, 0), memory_space=pltpu.SMEM),
                  pl.BlockSpec((1, TOP_K, SUB), lambda i: (i, 0, 0)), pl.BlockSpec((1, TOP_K, SUB), lambda i: (i, 0, 0)),
                  tile(d),
                  pl.BlockSpec((d, ds), const2), pl.BlockSpec((d, ds), const2), pl.BlockSpec((ds, d), const2),
                  tile(d), pl.BlockSpec((1, N_MOD, d), lambda i: (i // per_b, 0, 0)), pl.BlockSpec((1, d), const2),
                  pl.BlockSpec(memory_space=pl.ANY)],
        out_specs=tile(d),
        out_shape=jax.ShapeDtypeStruct((t, d), f32),
        scratch_shapes=[pltpu.VMEM((2, NLOC, d), bf16), pltpu.VMEM((NLOC, SUB), bf16),
                        pltpu.SemaphoreType.DMA((2,)), pltpu.SMEM((2,), jnp.int32)],
        compiler_params=_cparams("arbitrary"),
        name="combine",
    )(meta, meta, posl, gatel, h2, wsg, wsu, wsd, x1, mod, g_post, ys)


def _tiles(s, sc):
    tm = math.gcd(s, 512)
    assert sc % tm == 0 or tm % sc == 0
    return dict(tm=tm, tm_ctx=math.gcd(sc, tm), tq=math.gcd(s, 512))


def _dispatch_tables(cnt, nt, ne, rows_cap):
    n = cnt.reshape(nt, ne, LANES)[:, :, 0].astype(jnp.int32)
    c = (n + (ROW_ALIGN - 1)) // ROW_ALIGN * ROW_ALIGN
    n_e = jnp.sum(c, axis=0)
    p_e = (n_e + (GMM_BLOCK - 1)) // GMM_BLOCK * GMM_BLOCK
    seg_end = jnp.cumsum(p_e)
    seg_start = seg_end - p_e
    g = seg_start[None, :] + jnp.cumsum(c, axis=0) - c
    meta = jnp.stack([c, g], axis=1)
    blk_start = jnp.arange(rows_cap // GMM_BLOCK, dtype=jnp.int32) * GMM_BLOCK
    be = jnp.minimum(jnp.sum(blk_start[:, None] >= seg_end[None, :], axis=1), ne - 1).astype(jnp.int32)
    nv = jnp.clip(n_e[be] - (blk_start - seg_start[be]), 0, GMM_BLOCK).astype(jnp.int32)
    nb = jnp.maximum(seg_end[-1:] // GMM_BLOCK, 1).astype(jnp.int32)
    return meta, be, nv, nb


def _rope_tables(s):
    n_freq = HEAD_DIM // 4
    inv_freq = ROPE_THETA ** (-jnp.arange(n_freq, dtype=f32) / n_freq)
    pos = jnp.arange(s)
    ang_r = (pos // GRID_W).astype(f32)[:, None] * inv_freq
    ang_c = (pos % GRID_W).astype(f32)[:, None] * inv_freq
    cr, sr, cc, sc_ = jnp.cos(ang_r), jnp.sin(ang_r), jnp.cos(ang_c), jnp.sin(ang_c)
    cos = jnp.concatenate([cr, cr, cc, cc] * (LANES // HEAD_DIM), axis=1)
    sin = jnp.concatenate([-sr, sr, -sc_, sc_] * (LANES // HEAD_DIM), axis=1)
    return cos, sin


def kernel(x, c, ctx, c_ctx, w_mod, b_mod, g_pre_mix, g_post_mix, w_in, q_norm_g, k_norm_g, lambda_q1, lambda_k1,
           lambda_q2, lambda_k2, subln_g, w_out, g_pre_ffn, g_post_ffn, w_router, router_bias, w_exp_gate, w_exp_up,
           w_exp_down, w_sh_gate, w_sh_up, w_sh_down):
    assert w_mod.shape[0] == 1, "single-layer block"
    b, s, d = x.shape
    sc = ctx.shape[1]
    sk = s + sc
    ne = w_router.shape[2]
    assert w_in.shape[2] == IN_WIDTH and ne % N_GROUPS == 0 and ne <= LANES
    tl = _tiles(s, sc)

    pad = (-(b + 1)) % 8
    cc = jnp.concatenate([c, c_ctx[None], jnp.zeros((pad, d), f32)], axis=0)
    mod_all = _mod(cc, w_mod[0], b_mod[0])
    mod = mod_all[:b].reshape(b, N_MOD, d)
    mod_c = mod_all[b:b + 1].reshape(1, N_MOD, d)

    w_in_b = w_in[0].astype(bf16)
    qg = jnp.tile(q_norm_g[0], GQA_HEADS)[None]
    kg = jnp.tile(k_norm_g[0], GQA_KV_HEADS)[None]
    head_id = jnp.arange(GQ_W) // HEAD_DIM
    bd = (head_id[:, None] == head_id[None, :]).astype(bf16)
    cos, sin = _rope_tables(s)
    g_pre = g_pre_mix[0][None]

    dq, gq, dk, dv, gk, gv = _inproj(x, mod, g_pre, w_in_b, qg, kg, bd, cos, sin, None,
                                     latent=True, sk=sk, row_off=0, tm=tl["tm"])
    dk, dv, gk, gv = _inproj(ctx, mod_c, g_pre, w_in_b, qg, kg, bd, cos, sin, (dk, dv, gk, gv),
                             latent=False, sk=sk, row_off=s, tm=tl["tm_ctx"])

    lamv = jnp.stack([lambda_q1[0], lambda_k1[0], lambda_q2[0], lambda_k2[0]])
    mixed = _attention(dq, gq, dk, dv, gk, gv, lamv, subln_g[0][None], tq=tl["tq"])

    x1, h2, st = _outproj(mixed, x, mod, w_out[0].astype(bf16), g_post_mix[0][None], g_pre_ffn[0][None],
                          w_router[0].T.astype(bf16), tm=tl["tm"])
    t = b * s
    nt = t // SUB
    assert s % SUB == 0 and ne <= N_EXPERTS_MAX
    cnt, posl, gatel = _route(st, router_bias[0][:, None])
    rows_cap = -(-(t * TOP_K + nt * ne * (ROW_ALIGN - 1) + ne * (GMM_BLOCK - 1)) // GMM_BLOCK) * GMM_BLOCK
    meta, be, nv, nb = _dispatch_tables(cnt, nt, ne, rows_cap)
    h2f = h2.reshape(t, d)
    xs = _dispatch(meta, posl, h2f, rows_cap)
    ys = _gmm(be, nv, nb, xs, w_exp_gate[0].astype(bf16), w_exp_up[0].astype(bf16), w_exp_down[0].astype(bf16))
    out = _combine(meta, posl, gatel, h2f, w_sh_gate[0].astype(bf16), w_sh_up[0].astype(bf16), w_sh_down[0].astype(bf16),
                   x1.reshape(t, d), mod, g_post_ffn[0][None], ys, s=s)
    return out.reshape(b, s, d)
```

```python
import functools
import math

import jax
import jax.numpy as jnp
from jax import lax
from jax.experimental import pallas as pl
from jax.experimental.pallas import tpu as pltpu

HEAD_DIM = 64
DIFF_HEADS = 4
GQA_HEADS = 8
GQA_KV_HEADS = 2
GRID_W = 64
ROPE_THETA = 10000.0
N_GROUPS = 8
TOPK_GROUPS = 4
TOP_K = 8
ROUTED_SCALE = 2.5
N_MOD = 6
EPS = 1e-6
LAM_INIT = 0.8 - 0.6 * math.exp(-0.3 * 0)

LANES = 128
DQ_W = DIFF_HEADS * 2 * HEAD_DIM
DV_W = DIFF_HEADS * 2 * HEAD_DIM
GQ_W = GQA_HEADS * HEAD_DIM
GK_W = GQA_KV_HEADS * HEAD_DIM
OFF_DQ, OFF_DK, OFF_DV = 0, DQ_W, 2 * DQ_W
OFF_GQ = OFF_DV + DV_W
OFF_GK = OFF_GQ + GQ_W
OFF_GV = OFF_GK + GK_W
IN_WIDTH = OFF_GV + GK_W

VMEM_LIMIT = 56 * 1024 * 1024

N_EXPERTS_MAX = 64
SUB = 256
ROW_ALIGN = 16
GMM_BLOCK = 1024
NLOC = -(-(SUB * TOP_K + N_EXPERTS_MAX * ROW_ALIGN) // SUB) * SUB
NLOC_MIN = SUB * TOP_K // SUB * SUB


def _swiglu_hidden(h, wg, wu):
    return jax.nn.silu(jnp.dot(h, wg, preferred_element_type=f32)) * jnp.dot(h, wu, preferred_element_type=f32)

f32 = jnp.float32
bf16 = jnp.bfloat16
NT_DIMS = (((1,), (1,)), ((), ()))
TN_DIMS = (((0,), (0,)), ((), ()))


def _cparams(*sem):
    return pltpu.CompilerParams(dimension_semantics=sem, vmem_limit_bytes=VMEM_LIMIT)


def _rms(x):
    return x * lax.rsqrt(jnp.mean(x * x, axis=-1, keepdims=True) + EPS)


def _mod_kernel(c_ref, w_ref, b_ref, o_ref):
    o_ref[...] = jnp.dot(jax.nn.silu(c_ref[...]), w_ref[...], preferred_element_type=f32) + b_ref[...]


def _mod(cc, w_mod, b_mod):
    rows, d = cc.shape
    n = w_mod.shape[1]
    bn = d
    return pl.pallas_call(
        _mod_kernel,
        grid=(n // bn,),
        in_specs=[pl.BlockSpec((rows, d), lambda j: (0, 0)),
                  pl.BlockSpec((d, bn), lambda j: (0, j)),
                  pl.BlockSpec((1, bn), lambda j: (0, j))],
        out_specs=pl.BlockSpec((rows, bn), lambda j: (0, j)),
        out_shape=jax.ShapeDtypeStruct((rows, n), f32),
        compiler_params=_cparams("arbitrary"),
        name="mod",
    )(cc, w_mod, b_mod.reshape(1, n))


def _rope_tile(xt, cos, sin_signed, lane_lo):
    partner = jnp.where(lane_lo, pltpu.roll(xt, LANES - 16, 1), pltpu.roll(xt, 16, 1))
    return xt * cos + partner * sin_signed


def _inproj_kernel(x_ref, mod_ref, g_ref, w_ref, qg_ref, kg_ref, bd_ref, cos_ref, sin_ref, *out_refs,
                   latent):
    x = x_ref[0]
    h = _rms(x) * g_ref[...]
    h = h * (1.0 + mod_ref[0, 1:2, :]) + mod_ref[0, 0:1, :]
    hb = h.astype(bf16)
    tm = x.shape[0]

    def proj(off, width):
        return jnp.dot(hb, w_ref[:, off:off + width], preferred_element_type=f32)

    def head_rms(v, gain):
        w = v.shape[1]
        ss = jnp.dot((v * v).astype(bf16), bd_ref[0:w, 0:w], preferred_element_type=f32)
        return v * lax.rsqrt(ss * (1.0 / HEAD_DIM) + EPS) * gain

    if latent:
        dq_ref, gq_ref, dk_ref, dv_ref, gk_ref, gv_ref = out_refs
        cos, sin = cos_ref[...], sin_ref[...]
        lane_lo = (lax.broadcasted_iota(jnp.int32, (tm, LANES), 1) % 32) < 16
        rope = lambda v: _rope_tile(v, cos, sin, lane_lo)
    else:
        dk_ref, dv_ref, gk_ref, gv_ref = out_refs[-4:]
        rope = lambda v: v

    def store(ref, v, fn):
        for t in range(v.shape[1] // LANES):
            sl = slice(LANES * t, LANES * (t + 1))
            ref[0, :, sl] = fn(v[:, sl]).astype(ref.dtype)

    qscale = HEAD_DIM ** -0.5 * math.log2(math.e)
    if latent:
        store(dq_ref, proj(OFF_DQ, DQ_W) * qscale, rope)
        store(gq_ref, head_rms(proj(OFF_GQ, GQ_W), qg_ref[...] * qscale), rope)
    store(dk_ref, proj(OFF_DK, DQ_W), rope)
    store(gk_ref, head_rms(proj(OFF_GK, GK_W), kg_ref[...]), rope)
    dv_ref[0] = proj(OFF_DV, DV_W).astype(bf16)
    gv_ref[0] = proj(OFF_GV, GK_W).astype(bf16)


def _inproj(x, mod, g_pre, w_in_b, qg, kg, bd, cos, sin, kv_bufs, *, latent, sk, row_off, tm):
    b, s, d = x.shape
    nt = s // tm
    kv_blk = row_off // tm
    kv_widths = (DQ_W, DV_W, GK_W, GK_W)
    const2 = lambda bi, i: (0, 0)
    in_specs = [
        pl.BlockSpec((1, tm, d), lambda bi, i: (bi, i, 0)),
        pl.BlockSpec((1, N_MOD, d), (lambda bi, i: (bi, 0, 0)) if latent else (lambda bi, i: (0, 0, 0))),
        pl.BlockSpec((1, d), const2),
        pl.BlockSpec((d, IN_WIDTH), const2),
        pl.BlockSpec((1, GQ_W), const2),
        pl.BlockSpec((1, GK_W), const2),
        pl.BlockSpec((GQ_W, GQ_W), const2),
        pl.BlockSpec((tm, LANES), (lambda bi, i: (i, 0)) if latent else const2),
        pl.BlockSpec((tm, LANES), (lambda bi, i: (i, 0)) if latent else const2),
    ]
    kv_specs = [pl.BlockSpec((1, tm, w), lambda bi, i: (bi, kv_blk + i, 0)) for w in kv_widths]
    kv_shapes = [jax.ShapeDtypeStruct((b, sk, w), bf16) for w in kv_widths]
    args = [x, mod, g_pre, w_in_b, qg, kg, bd, cos, sin]
    if latent:
        out_specs = [pl.BlockSpec((1, tm, DQ_W), lambda bi, i: (bi, i, 0)),
                     pl.BlockSpec((1, tm, GQ_W), lambda bi, i: (bi, i, 0))] + kv_specs
        out_shape = [jax.ShapeDtypeStruct((b, s, DQ_W), bf16), jax.ShapeDtypeStruct((b, s, GQ_W), bf16)] + kv_shapes
        aliases = {}
    else:
        in_specs += [pl.BlockSpec(memory_space=pl.ANY)] * 4
        args += list(kv_bufs)
        out_specs, out_shape = kv_specs, kv_shapes
        aliases = {len(args) - 4 + k: k for k in range(4)}
    return pl.pallas_call(
        functools.partial(_inproj_kernel, latent=latent),
        grid=(b, nt),
        in_specs=in_specs,
        out_specs=out_specs,
        out_shape=out_shape,
        input_output_aliases=aliases,
        compiler_params=_cparams("parallel", "arbitrary"),
        name="inproj_latent" if latent else "inproj_ctx",
    )(*args)


def _softmax_pv(q, k, vext):
    s = lax.dot_general(q, k, NT_DIMS, preferred_element_type=f32)
    p = jnp.exp2(s - jnp.max(s, axis=-1, keepdims=True)).astype(bf16)
    r = jnp.dot(p, vext, preferred_element_type=f32)
    return r[:, :LANES] / r[:, LANES:]


def _attn_kernel(dq_ref, gq_ref, dk_ref, dv_ref, gk_ref, gv_ref, lam_ref, sg_ref, o_ref, vext_ref):
    tq = dq_ref.shape[1]
    sk = dk_ref.shape[1]

    @pl.when(pl.program_id(1) == 0)
    def _():
        ones = jnp.ones((sk, LANES), bf16)
        for h in range(DIFF_HEADS):
            vext_ref[h, :, :LANES] = dv_ref[0, :, LANES * h:LANES * (h + 1)]
            vext_ref[h, :, LANES:] = ones
        vext_ref[DIFF_HEADS, :, :LANES] = gv_ref[0]
        vext_ref[DIFF_HEADS, :, LANES:] = ones

    lane = lax.broadcasted_iota(jnp.int32, (tq, LANES), 1)
    lo = lane < HEAD_DIM
    lv = lam_ref[...]
    lam = (jnp.exp(jnp.sum(lv[0:1] * lv[1:2], axis=1, keepdims=True))
           - jnp.exp(jnp.sum(lv[2:3] * lv[3:4], axis=1, keepdims=True)) + LAM_INIT)
    zero = jnp.zeros((), bf16)

    for h in range(DIFF_HEADS):
        sl = slice(LANES * h, LANES * (h + 1))
        qt = dq_ref[0, :, sl]
        kt = dk_ref[0, :, sl]
        o1 = _softmax_pv(jnp.where(lo, qt, zero), kt, vext_ref[h])
        o2 = _softmax_pv(jnp.where(lo, zero, qt), kt, vext_ref[h])
        o = _rms(o1 - lam * o2) * (sg_ref[...] * (1.0 - LAM_INIT))
        o_ref[0, :, sl] = o.astype(o_ref.dtype)

    group = GQA_HEADS // GQA_KV_HEADS
    kt = gk_ref[0]
    for t in range(GQA_HEADS // 2):
        halves = []
        for pos in range(2):
            hd = 2 * t + pos
            j = hd // group
            qt = gq_ref[0, :, LANES * t:LANES * (t + 1)]
            if pos != j:
                qt = pltpu.roll(qt, HEAD_DIM, 1)
            o = _softmax_pv(jnp.where(lo if j == 0 else ~lo, qt, zero), kt, vext_ref[DIFF_HEADS])
            if pos != j:
                o = pltpu.roll(o, HEAD_DIM, 1)
            halves.append(o)
        off = DIFF_HEADS * 2 * HEAD_DIM + LANES * t
        o_ref[0, :, off:off + LANES] = jnp.where(lo, halves[0], halves[1]).astype(o_ref.dtype)


def _attention(dq, gq, dk, dv, gk, gv, lamv, subln_g, *, tq):
    b, s, _ = dq.shape
    sk = dk.shape[1]
    qspec = lambda w: pl.BlockSpec((1, tq, w), lambda bi, qi: (bi, qi, 0))
    kspec = lambda w: pl.BlockSpec((1, sk, w), lambda bi, qi: (bi, 0, 0))
    const2 = lambda bi, qi: (0, 0)
    mix_w = DV_W + GQ_W
    return pl.pallas_call(
        _attn_kernel,
        grid=(b, s // tq),
        in_specs=[qspec(DQ_W), qspec(GQ_W), kspec(DQ_W), kspec(DV_W), kspec(GK_W), kspec(GK_W),
                  pl.BlockSpec((4, HEAD_DIM), const2), pl.BlockSpec((1, 2 * HEAD_DIM), const2)],
        out_specs=pl.BlockSpec((1, tq, mix_w), lambda bi, qi: (bi, qi, 0)),
        out_shape=jax.ShapeDtypeStruct((b, s, mix_w), bf16),
        scratch_shapes=[pltpu.VMEM((DIFF_HEADS + 1, sk, 2 * LANES), bf16)],
        compiler_params=_cparams("parallel", "arbitrary"),
        name="attn",
    )(dq, gq, dk, dv, gk, gv, lamv, subln_g)


def _outproj_kernel(mix_ref, x_ref, mod_ref, wo_ref, gpost_ref, gpre_ref, wr_ref, x1_ref, h2_ref, st_ref):
    o = jnp.dot(mix_ref[0], wo_ref[...], preferred_element_type=f32)
    x1 = x_ref[0] + mod_ref[0, 2:3, :] * (_rms(o) * gpost_ref[...])
    x1_ref[0] = x1
    h2 = _rms(x1) * gpre_ref[...]
    h2 = (h2 * (1.0 + mod_ref[0, 4:5, :]) + mod_ref[0, 3:4, :]).astype(bf16)
    h2_ref[0] = h2
    st_ref[...] = jax.nn.sigmoid(lax.dot_general(wr_ref[...], h2, NT_DIMS, preferred_element_type=f32))


def _outproj(mixed, x, mod, w_out_b, g_post, g_pre, wr_t, *, tm):
    b, s, d = x.shape
    nt = s // tm
    ne = wr_t.shape[0]
    mw = mixed.shape[2]
    const2 = lambda bi, i: (0, 0)
    tile = lambda w: pl.BlockSpec((1, tm, w), lambda bi, i: (bi, i, 0))
    return pl.pallas_call(
        _outproj_kernel,
        grid=(b, nt),
        in_specs=[tile(mw), tile(d), pl.BlockSpec((1, N_MOD, d), lambda bi, i: (bi, 0, 0)),
                  pl.BlockSpec((mw, d), const2), pl.BlockSpec((1, d), const2), pl.BlockSpec((1, d), const2),
                  pl.BlockSpec((ne, d), const2)],
        out_specs=[tile(d), tile(d), pl.BlockSpec((ne, tm), lambda bi, i: (0, bi * nt + i))],
        out_shape=[jax.ShapeDtypeStruct((b, s, d), f32), jax.ShapeDtypeStruct((b, s, d), bf16),
                   jax.ShapeDtypeStruct((ne, b * s), f32)],
        compiler_params=_cparams("parallel", "arbitrary"),
        name="outproj",
    )(mixed, x, mod, w_out_b, g_post, g_pre, wr_t)


def _first_argmax(v, iota, n):
    m = jnp.max(v, axis=0, keepdims=True)
    idx = jnp.min(jnp.where(v == m, iota, n), axis=0, keepdims=True)
    return m, idx


def _route_gates_t(s, bias):
    ne, tm = s.shape
    gsz = ne // N_GROUPS
    neg = -jnp.inf
    ssel = s + bias
    sub = lax.broadcasted_iota(jnp.int32, (gsz, tm), 0).astype(f32)
    gscore = []
    for g in range(N_GROUPS):
        blk = ssel[gsz * g:gsz * (g + 1)]
        m1, i1 = _first_argmax(blk, sub, float(gsz))
        m2 = jnp.max(jnp.where(sub == i1, neg, blk), axis=0, keepdims=True)
        gscore.append(m1 + m2)
    gs = jnp.concatenate(gscore, axis=0)
    giota = lax.broadcasted_iota(jnp.int32, (N_GROUPS, tm), 0).astype(f32)
    gsel = jnp.zeros((N_GROUPS, tm), f32)
    for _ in range(TOPK_GROUPS):
        _, gi = _first_argmax(gs, giota, float(N_GROUPS))
        hit = giota == gi
        gsel = jnp.where(hit, 1.0, gsel)
        gs = jnp.where(hit, neg, gs)
    emask = jnp.concatenate([jnp.broadcast_to(gsel[g:g + 1], (gsz, tm)) for g in range(N_GROUPS)], axis=0)
    cand = jnp.where(emask > 0.0, ssel, neg)
    eiota = lax.broadcasted_iota(jnp.int32, (ne, tm), 0).astype(f32)
    w = jnp.zeros((ne, tm), f32)
    sel = jnp.zeros((ne, tm), f32)
    for _ in range(TOP_K):
        _, ei = _first_argmax(cand, eiota, float(ne))
        hit = eiota == ei
        w = jnp.where(hit, s, w)
        sel = jnp.where(hit, 1.0, sel)
        cand = jnp.where(hit, neg, cand)
    return w / jnp.sum(w, axis=0, keepdims=True) * ROUTED_SCALE, sel


def _route_kernel(st_ref, bias_ref, cnt_ref, posl_ref, gate_ref):
    gates, sel = _route_gates_t(st_ref[...], bias_ref[...])
    ne, sub = sel.shape
    n = jnp.sum(sel, axis=1, keepdims=True)
    cnt_ref[...] = jnp.broadcast_to(n, (ne, LANES))
    c = jnp.maximum(jnp.floor((n + (ROW_ALIGN - 1)) * (1.0 / ROW_ALIGN)), 1.0) * ROW_ALIGN
    lower = (lax.broadcasted_iota(jnp.int32, (ne, ne), 1) < lax.broadcasted_iota(jnp.int32, (ne, ne), 0)).astype(f32)
    base = jnp.dot(lower, jnp.broadcast_to(c, (ne, LANES)), preferred_element_type=f32)[:, 0:1]
    upper = (lax.broadcasted_iota(jnp.int32, (sub, sub), 0) < lax.broadcasted_iota(jnp.int32, (sub, sub), 1)).astype(bf16)
    rank = jnp.dot(sel.astype(bf16), upper, preferred_element_type=f32)
    pos = base + rank
    eiota = lax.broadcasted_iota(jnp.int32, (ne, sub), 0).astype(f32)
    rem = sel
    pos_rows, w_rows = [], []
    for _ in range(TOP_K):
        ei = jnp.min(jnp.where(rem > 0.0, eiota, float(ne)), axis=0, keepdims=True)
        hit = eiota == ei
        pos_rows.append(jnp.sum(jnp.where(hit, pos, 0.0), axis=0, keepdims=True))
        w_rows.append(jnp.sum(jnp.where(hit, gates, 0.0), axis=0, keepdims=True))
        rem = jnp.where(hit, 0.0, rem)
    posl_ref[0] = jnp.concatenate(pos_rows, axis=0)
    gate_ref[0] = jnp.concatenate(w_rows, axis=0)


def _route(st, bias):
    ne, t = st.shape
    nt = t // SUB
    return pl.pallas_call(
        _route_kernel,
        grid=(nt,),
        in_specs=[pl.BlockSpec((ne, SUB), lambda i: (0, i)), pl.BlockSpec((ne, 1), lambda i: (0, 0))],
        out_specs=[pl.BlockSpec((ne, LANES), lambda i: (i, 0)),
                   pl.BlockSpec((1, TOP_K, SUB), lambda i: (i, 0, 0)),
                   pl.BlockSpec((1, TOP_K, SUB), lambda i: (i, 0, 0))],
        out_shape=[jax.ShapeDtypeStruct((nt * ne, LANES), f32),
                   jax.ShapeDtypeStruct((nt, TOP_K, SUB), f32),
                   jax.ShapeDtypeStruct((nt, TOP_K, SUB), f32)],
        compiler_params=_cparams("parallel"),
        name="route",
    )(st, bias)


def _rows_copy(loc_ref, hbm_ref, sem, slot, lo, go, rows, to_hbm):
    src = loc_ref.at[slot, pl.ds(lo, rows)]
    dst = hbm_ref.at[pl.ds(go, rows)]
    return pltpu.make_async_copy(src, dst, sem.at[slot]) if to_hbm else pltpu.make_async_copy(dst, src, sem.at[slot])


def _start_chunks(meta_ref, loc_ref, hbm_ref, sem, slot, to_hbm):
    def per_expert(e, base):
        c = pl.multiple_of(meta_ref[0, 0, e], ROW_ALIGN)
        g = pl.multiple_of(meta_ref[0, 1, e], ROW_ALIGN)
        _rows_copy(loc_ref, hbm_ref, sem, slot, pl.multiple_of(base, ROW_ALIGN), g, c, to_hbm).start()
        return base + c

    return lax.fori_loop(0, meta_ref.shape[2], per_expert, 0, unroll=8)


def _wait_rows(rows, loc_ref, hbm_ref, sem, slot, to_hbm):
    units = rows // ROW_ALIGN
    for bit in range((NLOC // ROW_ALIGN).bit_length()):
        @pl.when((units >> bit) & 1 == 1)
        def _():
            _rows_copy(loc_ref, hbm_ref, sem, slot, 0, 0, ROW_ALIGN << bit, to_hbm).wait()


def _select_chunk(posl, vals, r, sub):
    rel = posl - float(r * sub)
    rel = jnp.where((rel >= 0.0) & (rel < float(sub)), rel, -1.0).astype(bf16)
    piota = lax.broadcasted_iota(jnp.int32, (sub, sub), 0).astype(f32).astype(bf16)
    m = jnp.zeros((sub, sub), bf16)
    for k in range(TOP_K):
        val = jnp.ones((), bf16) if vals is None else vals[k:k + 1]
        m = jnp.where(piota == rel[k:k + 1], val, m)
    return m


def _dispatch_kernel(meta_ref, posl_ref, h_ref, xs_ref, loc_ref, sem, nrows):
    i = pl.program_id(0)
    last = pl.num_programs(0) - 1
    slot = i % 2
    sub = h_ref.shape[0]

    @pl.when(i >= 2)
    def _():
        _wait_rows(nrows[slot], loc_ref, xs_ref, sem, slot, True)

    def tile_rows(e, acc):
        return acc + meta_ref[0, 0, e]

    tot = lax.fori_loop(0, meta_ref.shape[2], tile_rows, 0, unroll=8)
    h = h_ref[...]
    posl = posl_ref[0]

    def sort_chunk(r):
        rows = jnp.dot(_select_chunk(posl, None, r, sub), h, preferred_element_type=f32)
        loc_ref[slot, r * sub:(r + 1) * sub, :] = rows.astype(loc_ref.dtype)

    always = (NLOC_MIN + NLOC) // (2 * sub)
    for r in range(always):
        sort_chunk(r)
    for r in range(always, NLOC // sub):
        pl.when(r * sub < tot)(functools.partial(sort_chunk, r))

    nrows[slot] = _start_chunks(meta_ref, loc_ref, xs_ref, sem, slot, True)

    @pl.when(i == last)
    def _():
        _wait_rows(nrows[slot], loc_ref, xs_ref, sem, slot, True)

        @pl.when(i >= 1)
        def _():
            _wait_rows(nrows[1 - slot], loc_ref, xs_ref, sem, 1 - slot, True)


def _dispatch(meta, posl, h2, rows_cap):
    t, d = h2.shape
    nt = t // SUB
    ne = meta.shape[2]
    return pl.pallas_call(
        _dispatch_kernel,
        grid=(nt,),
        in_specs=[pl.BlockSpec((1, 2, ne), lambda i: (i, 0, 0), memory_space=pltpu.SMEM),
                  pl.BlockSpec((1, TOP_K, SUB), lambda i: (i, 0, 0)),
                  pl.BlockSpec((SUB, d), lambda i: (i, 0))],
        out_specs=pl.BlockSpec(memory_space=pl.ANY),
        out_shape=jax.ShapeDtypeStruct((rows_cap, d), bf16),
        scratch_shapes=[pltpu.VMEM((2, NLOC, d), bf16), pltpu.SemaphoreType.DMA((2,)), pltpu.SMEM((2,), jnp.int32)],
        compiler_params=_cparams("arbitrary"),
        name="dispatch",
    )(meta, posl, h2)


def _gmm_kernel(be_ref, nv_ref, nb_ref, x_ref, wg_ref, wu_ref, wd_ref, y_ref):
    i = pl.program_id(0)
    nv = nv_ref[i]

    @pl.when(nv > 0)
    def _():
        x = x_ref[...]
        row = lax.broadcasted_iota(jnp.int32, x.shape, 0)
        x = jnp.where(row < nv, x, jnp.zeros((), x.dtype))
        a = _swiglu_hidden(x, wg_ref[0].astype(bf16), wu_ref[0].astype(bf16))
        y_ref[...] = jnp.dot(a.astype(bf16), wd_ref[0].astype(bf16), preferred_element_type=f32).astype(y_ref.dtype)


def _gmm(be, nv, nb, xs, weg, weu, wed):
    rows_cap, d = xs.shape
    de = weg.shape[2]
    blk = lambda i, be, nv, nb: (jnp.minimum(i, nb[0] - 1), 0)
    wsel = lambda i, be, nv, nb: (be[i], 0, 0)
    return pl.pallas_call(
        _gmm_kernel,
        grid_spec=pltpu.PrefetchScalarGridSpec(
            num_scalar_prefetch=3,
            grid=(rows_cap // GMM_BLOCK,),
            in_specs=[pl.BlockSpec((GMM_BLOCK, d), blk),
                      pl.BlockSpec((1, d, de), wsel), pl.BlockSpec((1, d, de), wsel), pl.BlockSpec((1, de, d), wsel)],
            out_specs=pl.BlockSpec((GMM_BLOCK, d), blk)),
        out_shape=jax.ShapeDtypeStruct((rows_cap, d), bf16),
        compiler_params=_cparams("arbitrary"),
        name="gmm",
    )(be, nv, nb, xs, weg, weu, wed)


def _combine_kernel(meta_ref, metan_ref, posl_ref, gate_ref, h_ref, wsg_ref, wsu_ref, wsd_ref, x1_ref, mod_ref, gp_ref,
                    ys_ref, o_ref, loc_ref, w_ref, sem, nrows):
    i = pl.program_id(0)
    last = pl.num_programs(0) - 1
    slot = i % 2
    sub = h_ref.shape[0]

    @pl.when(i == 0)
    def _():
        loc_ref[...] = jnp.zeros(loc_ref.shape, loc_ref.dtype)
        nrows[0] = _start_chunks(meta_ref, loc_ref, ys_ref, sem, 0, False)

    @pl.when(i < last)
    def _():
        nrows[1 - slot] = _start_chunks(metan_ref, loc_ref, ys_ref, sem, 1 - slot, False)

    _wait_rows(nrows[slot], loc_ref, ys_ref, sem, slot, False)

    h = h_ref[...]
    a = _swiglu_hidden(h, wsg_ref[...], wsu_ref[...])
    y = jnp.dot(a.astype(bf16), wsd_ref[...], preferred_element_type=f32)

    posl = posl_ref[0]
    gates = gate_ref[0].astype(bf16)
    for r in range(NLOC // sub):
        w_ref[r * sub:(r + 1) * sub, :] = _select_chunk(posl, gates, r, sub)
    y = y + lax.dot_general(w_ref[...], loc_ref[slot], TN_DIMS, preferred_element_type=f32)
    o_ref[...] = x1_ref[...] + mod_ref[0, 5:6, :] * (_rms(y) * gp_ref[...])


def _combine(meta, posl, gatel, h2, wsg, wsu, wsd, x1, mod, g_post, ys, *, s):
    t, d = h2.shape
    nt = t // SUB
    ne = meta.shape[2]
    ds = wsg.shape[1]
    per_b = s // SUB
    tile = lambda w: pl.BlockSpec((SUB, w), lambda i: (i, 0))
    const2 = lambda i: (0, 0)
    return pl.pallas_call(
        _combine_kernel,
        grid=(nt,),
        in_specs=[pl.BlockSpec((1, 2, ne), lambda i: (i, 0, 0), memory_space=pltpu.SMEM),
                  pl.BlockSpec((1, 2, ne), lambda i: (jnp.minimum(i + 1, nt - 1), 0, 0), memory_space=pltpu.SMEM),
                  pl.BlockSpec((1, TOP_K, SUB), lambda i: (i, 0, 0)), pl.BlockSpec((1, TOP_K, SUB), lambda i: (i, 0, 0)),
                  tile(d),
                  pl.BlockSpec((d, ds), const2), pl.BlockSpec((d, ds), const2), pl.BlockSpec((ds, d), const2),
                  tile(d), pl.BlockSpec((1, N_MOD, d), lambda i: (i // per_b, 0, 0)), pl.BlockSpec((1, d), const2),
                  pl.BlockSpec(memory_space=pl.ANY)],
        out_specs=tile(d),
        out_shape=jax.ShapeDtypeStruct((t, d), f32),
        scratch_shapes=[pltpu.VMEM((2, NLOC, d), bf16), pltpu.VMEM((NLOC, SUB), bf16),
                        pltpu.SemaphoreType.DMA((2,)), pltpu.SMEM((2,), jnp.int32)],
        compiler_params=_cparams("arbitrary"),
        name="combine",
    )(meta, meta, posl, gatel, h2, wsg, wsu, wsd, x1, mod, g_post, ys)


def _tiles(s, sc):
    tm = math.gcd(s, 512)
    assert sc % tm == 0 or tm % sc == 0
    return dict(tm=tm, tm_ctx=math.gcd(sc, tm), tq=math.gcd(s, 512))


def _dispatch_tables(cnt, nt, ne, rows_cap):
    n = cnt.reshape(nt, ne, LANES)[:, :, 0].astype(jnp.int32)
    c = jnp.maximum((n + (ROW_ALIGN - 1)) // ROW_ALIGN, 1) * ROW_ALIGN
    n_e = jnp.sum(c, axis=0)
    p_e = (n_e + (GMM_BLOCK - 1)) // GMM_BLOCK * GMM_BLOCK
    seg_end = jnp.cumsum(p_e)
    seg_start = seg_end - p_e
    g = seg_start[None, :] + jnp.cumsum(c, axis=0) - c
    meta = jnp.stack([c, g], axis=1)
    blk_start = jnp.arange(rows_cap // GMM_BLOCK, dtype=jnp.int32) * GMM_BLOCK
    be = jnp.minimum(jnp.sum(blk_start[:, None] >= seg_end[None, :], axis=1), ne - 1).astype(jnp.int32)
    nv = jnp.clip(n_e[be] - (blk_start - seg_start[be]), 0, GMM_BLOCK).astype(jnp.int32)
    nb = jnp.maximum(seg_end[-1:] // GMM_BLOCK, 1).astype(jnp.int32)
    return meta, be, nv, nb


def _rope_tables(s):
    n_freq = HEAD_DIM // 4
    inv_freq = ROPE_THETA ** (-jnp.arange(n_freq, dtype=f32) / n_freq)
    pos = jnp.arange(s)
    ang_r = (pos // GRID_W).astype(f32)[:, None] * inv_freq
    ang_c = (pos % GRID_W).astype(f32)[:, None] * inv_freq
    cr, sr, cc, sc_ = jnp.cos(ang_r), jnp.sin(ang_r), jnp.cos(ang_c), jnp.sin(ang_c)
    cos = jnp.concatenate([cr, cr, cc, cc] * (LANES // HEAD_DIM), axis=1)
    sin = jnp.concatenate([-sr, sr, -sc_, sc_] * (LANES // HEAD_DIM), axis=1)
    return cos, sin


def kernel(x, c, ctx, c_ctx, w_mod, b_mod, g_pre_mix, g_post_mix, w_in, q_norm_g, k_norm_g, lambda_q1, lambda_k1,
           lambda_q2, lambda_k2, subln_g, w_out, g_pre_ffn, g_post_ffn, w_router, router_bias, w_exp_gate, w_exp_up,
           w_exp_down, w_sh_gate, w_sh_up, w_sh_down):
    assert w_mod.shape[0] == 1, "single-layer block"
    b, s, d = x.shape
    sc = ctx.shape[1]
    sk = s + sc
    ne = w_router.shape[2]
    assert w_in.shape[2] == IN_WIDTH and ne % N_GROUPS == 0 and ne <= LANES
    tl = _tiles(s, sc)

    pad = (-(b + 1)) % 8
    cc = jnp.concatenate([c, c_ctx[None], jnp.zeros((pad, d), f32)], axis=0)
    mod_all = _mod(cc, w_mod[0], b_mod[0])
    mod = mod_all[:b].reshape(b, N_MOD, d)
    mod_c = mod_all[b:b + 1].reshape(1, N_MOD, d)

    w_in_b = w_in[0].astype(bf16)
    qg = jnp.tile(q_norm_g[0], GQA_HEADS)[None]
    kg = jnp.tile(k_norm_g[0], GQA_KV_HEADS)[None]
    head_id = jnp.arange(GQ_W) // HEAD_DIM
    bd = (head_id[:, None] == head_id[None, :]).astype(bf16)
    cos, sin = _rope_tables(s)
    g_pre = g_pre_mix[0][None]

    dq, gq, dk, dv, gk, gv = _inproj(x, mod, g_pre, w_in_b, qg, kg, bd, cos, sin, None,
                                     latent=True, sk=sk, row_off=0, tm=tl["tm"])
    dk, dv, gk, gv = _inproj(ctx, mod_c, g_pre, w_in_b, qg, kg, bd, cos, sin, (dk, dv, gk, gv),
                             latent=False, sk=sk, row_off=s, tm=tl["tm_ctx"])

    lamv = jnp.stack([lambda_q1[0], lambda_k1[0], lambda_q2[0], lambda_k2[0]])
    mixed = _attention(dq, gq, dk, dv, gk, gv, lamv, subln_g[0][None], tq=tl["tq"])

    x1, h2, st = _outproj(mixed, x, mod, w_out[0].astype(bf16), g_post_mix[0][None], g_pre_ffn[0][None],
                          w_router[0].T.astype(bf16), tm=tl["tm"])
    t = b * s
    nt = t // SUB
    assert s % SUB == 0 and ne <= N_EXPERTS_MAX
    cnt, posl, gatel = _route(st, router_bias[0][:, None])
    rows_cap = -(-(t * TOP_K + nt * ne * ROW_ALIGN + ne * (GMM_BLOCK - 1)) // GMM_BLOCK) * GMM_BLOCK
    meta, be, nv, nb = _dispatch_tables(cnt, nt, ne, rows_cap)
    h2f = h2.reshape(t, d)
    xs = _dispatch(meta, posl, h2f, rows_cap)
    ys = _gmm(be, nv, nb, xs, w_exp_gate[0], w_exp_up[0], w_exp_down[0])
    out = _combine(meta, posl, gatel, h2f, w_sh_gate[0].astype(bf16), w_sh_up[0].astype(bf16), w_sh_down[0].astype(bf16),
                   x1.reshape(t, d), mod, g_post_ffn[0][None], ys, s=s)
    return out.reshape(b, s, d)
```

```python
import functools
import math

import jax
import jax.numpy as jnp
from jax import lax
from jax.experimental import pallas as pl
from jax.experimental.pallas import tpu as pltpu

HEAD_DIM = 64
DIFF_HEADS = 4
GQA_HEADS = 8
GQA_KV_HEADS = 2
GRID_W = 64
ROPE_THETA = 10000.0
N_GROUPS = 8
TOPK_GROUPS = 4
TOP_K = 8
ROUTED_SCALE = 2.5
N_MOD = 6
EPS = 1e-6
LAM_INIT = 0.8 - 0.6 * math.exp(-0.3 * 0)

LANES = 128
DQ_W = DIFF_HEADS * 2 * HEAD_DIM
DV_W = DIFF_HEADS * 2 * HEAD_DIM
GQ_W = GQA_HEADS * HEAD_DIM
GK_W = GQA_KV_HEADS * HEAD_DIM
OFF_DQ, OFF_DK, OFF_DV = 0, DQ_W, 2 * DQ_W
OFF_GQ = OFF_DV + DV_W
OFF_GK = OFF_GQ + GQ_W
OFF_GV = OFF_GK + GK_W
IN_WIDTH = OFF_GV + GK_W

VMEM_LIMIT = 56 * 1024 * 1024

N_EXPERTS_MAX = 64
SUB = 256
ROW_ALIGN = 16
VEXT_ROWS = LANES + ROW_ALIGN
MAX_SLAB_ROWS = 256
GMM_BLOCK = 1024
NLOC = -(-(SUB * TOP_K + N_EXPERTS_MAX * ROW_ALIGN) // SUB) * SUB
NLOC_MIN = SUB * TOP_K // SUB * SUB


def _swiglu_hidden(h, wg, wu):
    return jax.nn.silu(jnp.dot(h, wg, preferred_element_type=f32)) * jnp.dot(h, wu, preferred_element_type=f32)

f32 = jnp.float32
bf16 = jnp.bfloat16
NT_DIMS = (((1,), (1,)), ((), ()))
TN_DIMS = (((0,), (0,)), ((), ()))


def _cparams(*sem):
    return pltpu.CompilerParams(dimension_semantics=sem, vmem_limit_bytes=VMEM_LIMIT)


def _rms(x):
    return x * lax.rsqrt(jnp.mean(x * x, axis=-1, keepdims=True) + EPS)


def _mod_kernel(c_ref, w_ref, b_ref, o_ref):
    o_ref[...] = jnp.dot(jax.nn.silu(c_ref[...]), w_ref[...], preferred_element_type=f32) + b_ref[...]


def _mod(cc, w_mod, b_mod):
    rows, d = cc.shape
    n = w_mod.shape[1]
    bn = d
    return pl.pallas_call(
        _mod_kernel,
        grid=(n // bn,),
        in_specs=[pl.BlockSpec((rows, d), lambda j: (0, 0)),
                  pl.BlockSpec((d, bn), lambda j: (0, j)),
                  pl.BlockSpec((1, bn), lambda j: (0, j))],
        out_specs=pl.BlockSpec((rows, bn), lambda j: (0, j)),
        out_shape=jax.ShapeDtypeStruct((rows, n), f32),
        compiler_params=_cparams("arbitrary"),
        name="mod",
    )(cc, w_mod, b_mod.reshape(1, n))


def _rope_tile(xt, cos, sin_signed, lane_lo):
    partner = jnp.where(lane_lo, pltpu.roll(xt, LANES - 16, 1), pltpu.roll(xt, 16, 1))
    return xt * cos + partner * sin_signed


def _inproj_kernel(x_ref, mod_ref, g_ref, w_ref, qg_ref, kg_ref, bd_ref, cos_ref, sin_ref, *out_refs,
                   latent):
    x = x_ref[0]
    h = _rms(x) * g_ref[...]
    h = h * (1.0 + mod_ref[0, 1:2, :]) + mod_ref[0, 0:1, :]
    hb = h.astype(bf16)
    tm = x.shape[0]

    def proj(off, width):
        return jnp.dot(hb, w_ref[:, off:off + width], preferred_element_type=f32)

    def head_rms(v, gain):
        w = v.shape[1]
        ss = jnp.dot((v * v).astype(bf16), bd_ref[0:w, 0:w], preferred_element_type=f32)
        return v * lax.rsqrt(ss * (1.0 / HEAD_DIM) + EPS) * gain

    if latent:
        dq_ref, gq_ref, dk_ref, dv_ref, gk_ref, gv_ref = out_refs
        cos, sin = cos_ref[...], sin_ref[...]
        lane_lo = (lax.broadcasted_iota(jnp.int32, (tm, LANES), 1) % 32) < 16
        rope = lambda v: _rope_tile(v, cos, sin, lane_lo)
    else:
        dk_ref, dv_ref, gk_ref, gv_ref = out_refs[-4:]
        rope = lambda v: v

    def store(ref, v, fn):
        for t in range(v.shape[1] // LANES):
            sl = slice(LANES * t, LANES * (t + 1))
            ref[0, :, sl] = fn(v[:, sl]).astype(ref.dtype)

    qscale = HEAD_DIM ** -0.5 * math.log2(math.e)
    if latent:
        store(dq_ref, proj(OFF_DQ, DQ_W) * qscale, rope)
        store(gq_ref, head_rms(proj(OFF_GQ, GQ_W), qg_ref[...] * qscale), rope)
    store(dk_ref, proj(OFF_DK, DQ_W), rope)
    store(gk_ref, head_rms(proj(OFF_GK, GK_W), kg_ref[...]), rope)
    dv_ref[0] = proj(OFF_DV, DV_W).astype(bf16)
    gv_ref[0] = proj(OFF_GV, GK_W).astype(bf16)


def _inproj(x, mod, g_pre, w_in_b, qg, kg, bd, cos, sin, kv_bufs, *, latent, sk, row_off, tm):
    b, s, d = x.shape
    nt = s // tm
    kv_blk = row_off // tm
    kv_widths = (DQ_W, DV_W, GK_W, GK_W)
    const2 = lambda bi, i: (0, 0)
    in_specs = [
        pl.BlockSpec((1, tm, d), lambda bi, i: (bi, i, 0)),
        pl.BlockSpec((1, N_MOD, d), (lambda bi, i: (bi, 0, 0)) if latent else (lambda bi, i: (0, 0, 0))),
        pl.BlockSpec((1, d), const2),
        pl.BlockSpec((d, IN_WIDTH), const2),
        pl.BlockSpec((1, GQ_W), const2),
        pl.BlockSpec((1, GK_W), const2),
        pl.BlockSpec((GQ_W, GQ_W), const2),
        pl.BlockSpec((tm, LANES), (lambda bi, i: (i, 0)) if latent else const2),
        pl.BlockSpec((tm, LANES), (lambda bi, i: (i, 0)) if latent else const2),
    ]
    kv_specs = [pl.BlockSpec((1, tm, w), lambda bi, i: (bi, kv_blk + i, 0)) for w in kv_widths]
    kv_shapes = [jax.ShapeDtypeStruct((b, sk, w), bf16) for w in kv_widths]
    args = [x, mod, g_pre, w_in_b, qg, kg, bd, cos, sin]
    if latent:
        out_specs = [pl.BlockSpec((1, tm, DQ_W), lambda bi, i: (bi, i, 0)),
                     pl.BlockSpec((1, tm, GQ_W), lambda bi, i: (bi, i, 0))] + kv_specs
        out_shape = [jax.ShapeDtypeStruct((b, s, DQ_W), bf16), jax.ShapeDtypeStruct((b, s, GQ_W), bf16)] + kv_shapes
        aliases = {}
    else:
        in_specs += [pl.BlockSpec(memory_space=pl.ANY)] * 4
        args += list(kv_bufs)
        out_specs, out_shape = kv_specs, kv_shapes
        aliases = {len(args) - 4 + k: k for k in range(4)}
    return pl.pallas_call(
        functools.partial(_inproj_kernel, latent=latent),
        grid=(b, nt),
        in_specs=in_specs,
        out_specs=out_specs,
        out_shape=out_shape,
        input_output_aliases=aliases,
        compiler_params=_cparams("parallel", "arbitrary"),
        name="inproj_latent" if latent else "inproj_ctx",
    )(*args)


def _attn_kernel(dq_ref, gq_ref, dk_ref, dv_ref, gk_ref, gv_ref, lam_ref, sg_ref, o_ref, vext_ref, st_ref, pt_ref):
    tq = dq_ref.shape[1]
    sk = dk_ref.shape[1]

    @pl.when(pl.program_id(1) == 0)
    def _():
        ones = jnp.ones((VEXT_ROWS - LANES, sk), bf16)
        for h in range(DIFF_HEADS + 1):
            v = gv_ref[0] if h == DIFF_HEADS else dv_ref[0, :, LANES * h:LANES * (h + 1)]
            vext_ref[h, :LANES, :] = v.astype(f32).T.astype(bf16)
            vext_ref[h, LANES:, :] = ones

    lane = lax.broadcasted_iota(jnp.int32, (tq, LANES), 1)
    lo = lane < HEAD_DIM
    lv = lam_ref[...]
    lam = (jnp.exp(jnp.sum(lv[0:1] * lv[1:2], axis=1, keepdims=True))
           - jnp.exp(jnp.sum(lv[2:3] * lv[3:4], axis=1, keepdims=True)) + LAM_INIT)
    zero = jnp.zeros((), bf16)
    group = GQA_HEADS // GQA_KV_HEADS

    maps = [("diff", h, w) for h in range(DIFF_HEADS) for w in range(2)]
    maps += [("gqa", t, pos) for t in range(GQA_HEADS // 2) for pos in range(2)]

    def scores_t(kind, a, b):
        if kind == "diff":
            sl = slice(LANES * a, LANES * (a + 1))
            qt = dq_ref[0, :, sl]
            qm = jnp.where(lo, qt, zero) if b == 0 else jnp.where(lo, zero, qt)
            return lax.dot_general(dk_ref[0, :, sl], qm, NT_DIMS, preferred_element_type=f32)
        j = (2 * a + b) // group
        qt = gq_ref[0, :, LANES * a:LANES * (a + 1)]
        if b != j:
            qt = pltpu.roll(qt, HEAD_DIM, 1)
        qm = jnp.where(lo if j == 0 else ~lo, qt, zero)
        return lax.dot_general(gk_ref[0], qm, NT_DIMS, preferred_element_type=f32)

    def values(i):
        kind, a, b = maps[i]
        r = jnp.dot(vext_ref[a if kind == "diff" else DIFF_HEADS], pt_ref[i % 2], preferred_element_type=f32)
        o = (r[:LANES] / r[LANES:LANES + 1]).T
        if kind == "gqa" and b != (2 * a + b) // group:
            o = pltpu.roll(o, HEAD_DIM, 1)
        return o

    slab = math.gcd(sk, MAX_SLAB_ROWS)
    outs = []
    st_ref[0] = scores_t(*maps[0])
    for i in range(len(maps)):
        if i + 1 < len(maps):
            st_ref[(i + 1) % 2] = scores_t(*maps[i + 1])
        st = st_ref[i % 2]
        m = jnp.max(jnp.max(st.reshape(sk // slab, slab, tq), axis=1), axis=0, keepdims=True)
        pt_ref[i % 2] = jnp.exp2(st - m).astype(bf16)
        outs.append(values(i))

    for h in range(DIFF_HEADS):
        o = _rms(outs[2 * h] - lam * outs[2 * h + 1]) * (sg_ref[...] * (1.0 - LAM_INIT))
        o_ref[0, :, LANES * h:LANES * (h + 1)] = o.astype(o_ref.dtype)
    for t in range(GQA_HEADS // 2):
        base = 2 * DIFF_HEADS + 2 * t
        off = DIFF_HEADS * 2 * HEAD_DIM + LANES * t
        o_ref[0, :, off:off + LANES] = jnp.where(lo, outs[base], outs[base + 1]).astype(o_ref.dtype)


def _attention(dq, gq, dk, dv, gk, gv, lamv, subln_g, *, tq):
    b, s, _ = dq.shape
    sk = dk.shape[1]
    qspec = lambda w: pl.BlockSpec((1, tq, w), lambda bi, qi: (bi, qi, 0))
    kspec = lambda w: pl.BlockSpec((1, sk, w), lambda bi, qi: (bi, 0, 0))
    const2 = lambda bi, qi: (0, 0)
    mix_w = DV_W + GQ_W
    return pl.pallas_call(
        _attn_kernel,
        grid=(b, s // tq),
        in_specs=[qspec(DQ_W), qspec(GQ_W), kspec(DQ_W), kspec(DV_W), kspec(GK_W), kspec(GK_W),
                  pl.BlockSpec((4, HEAD_DIM), const2), pl.BlockSpec((1, 2 * HEAD_DIM), const2)],
        out_specs=pl.BlockSpec((1, tq, mix_w), lambda bi, qi: (bi, qi, 0)),
        out_shape=jax.ShapeDtypeStruct((b, s, mix_w), bf16),
        scratch_shapes=[pltpu.VMEM((DIFF_HEADS + 1, VEXT_ROWS, sk), bf16),
                        pltpu.VMEM((2, sk, tq), f32), pltpu.VMEM((2, sk, tq), bf16)],
        compiler_params=_cparams("parallel", "arbitrary"),
        name="attn",
    )(dq, gq, dk, dv, gk, gv, lamv, subln_g)


def _outproj_kernel(mix_ref, x_ref, mod_ref, wo_ref, gpost_ref, gpre_ref, wr_ref, x1_ref, h2_ref, st_ref):
    o = jnp.dot(mix_ref[0], wo_ref[...], preferred_element_type=f32)
    x1 = x_ref[0] + mod_ref[0, 2:3, :] * (_rms(o) * gpost_ref[...])
    x1_ref[0] = x1
    h2 = _rms(x1) * gpre_ref[...]
    h2 = (h2 * (1.0 + mod_ref[0, 4:5, :]) + mod_ref[0, 3:4, :]).astype(bf16)
    h2_ref[0] = h2
    st_ref[...] = jax.nn.sigmoid(lax.dot_general(wr_ref[...], h2, NT_DIMS, preferred_element_type=f32))


def _outproj(mixed, x, mod, w_out_b, g_post, g_pre, wr_t, *, tm):
    b, s, d = x.shape
    nt = s // tm
    ne = wr_t.shape[0]
    mw = mixed.shape[2]
    const2 = lambda bi, i: (0, 0)
    tile = lambda w: pl.BlockSpec((1, tm, w), lambda bi, i: (bi, i, 0))
    return pl.pallas_call(
        _outproj_kernel,
        grid=(b, nt),
        in_specs=[tile(mw), tile(d), pl.BlockSpec((1, N_MOD, d), lambda bi, i: (bi, 0, 0)),
                  pl.BlockSpec((mw, d), const2), pl.BlockSpec((1, d), const2), pl.BlockSpec((1, d), const2),
                  pl.BlockSpec((ne, d), const2)],
        out_specs=[tile(d), tile(d), pl.BlockSpec((ne, tm), lambda bi, i: (0, bi * nt + i))],
        out_shape=[jax.ShapeDtypeStruct((b, s, d), f32), jax.ShapeDtypeStruct((b, s, d), bf16),
                   jax.ShapeDtypeStruct((ne, b * s), f32)],
        compiler_params=_cparams("parallel", "arbitrary"),
        name="outproj",
    )(mixed, x, mod, w_out_b, g_post, g_pre, wr_t)


def _first_argmax(v, iota, n):
    m = jnp.max(v, axis=0, keepdims=True)
    idx = jnp.min(jnp.where(v == m, iota, n), axis=0, keepdims=True)
    return m, idx


def _route_gates_t(s, bias):
    ne, tm = s.shape
    gsz = ne // N_GROUPS
    neg = -jnp.inf
    ssel = s + bias
    sub = lax.broadcasted_iota(jnp.int32, (gsz, tm), 0).astype(f32)
    gscore = []
    for g in range(N_GROUPS):
        blk = ssel[gsz * g:gsz * (g + 1)]
        m1, i1 = _first_argmax(blk, sub, float(gsz))
        m2 = jnp.max(jnp.where(sub == i1, neg, blk), axis=0, keepdims=True)
        gscore.append(m1 + m2)
    gs = jnp.concatenate(gscore, axis=0)
    giota = lax.broadcasted_iota(jnp.int32, (N_GROUPS, tm), 0).astype(f32)
    gsel = jnp.zeros((N_GROUPS, tm), f32)
    for _ in range(TOPK_GROUPS):
        _, gi = _first_argmax(gs, giota, float(N_GROUPS))
        hit = giota == gi
        gsel = jnp.where(hit, 1.0, gsel)
        gs = jnp.where(hit, neg, gs)
    emask = jnp.concatenate([jnp.broadcast_to(gsel[g:g + 1], (gsz, tm)) for g in range(N_GROUPS)], axis=0)
    cand = jnp.where(emask > 0.0, ssel, neg)
    eiota = lax.broadcasted_iota(jnp.int32, (ne, tm), 0).astype(f32)
    w = jnp.zeros((ne, tm), f32)
    sel = jnp.zeros((ne, tm), f32)
    for _ in range(TOP_K):
        _, ei = _first_argmax(cand, eiota, float(ne))
        hit = eiota == ei
        w = jnp.where(hit, s, w)
        sel = jnp.where(hit, 1.0, sel)
        cand = jnp.where(hit, neg, cand)
    return w / jnp.sum(w, axis=0, keepdims=True) * ROUTED_SCALE, sel


def _route_kernel(st_ref, bias_ref, cnt_ref, posl_ref, gate_ref):
    gates, sel = _route_gates_t(st_ref[...], bias_ref[...])
    ne, sub = sel.shape
    n = jnp.sum(sel, axis=1, keepdims=True)
    cnt_ref[...] = jnp.broadcast_to(n, (ne, LANES))
    c = jnp.maximum(jnp.floor((n + (ROW_ALIGN - 1)) * (1.0 / ROW_ALIGN)), 1.0) * ROW_ALIGN
    lower = (lax.broadcasted_iota(jnp.int32, (ne, ne), 1) < lax.broadcasted_iota(jnp.int32, (ne, ne), 0)).astype(f32)
    base = jnp.dot(lower, jnp.broadcast_to(c, (ne, LANES)), preferred_element_type=f32)[:, 0:1]
    upper = (lax.broadcasted_iota(jnp.int32, (sub, sub), 0) < lax.broadcasted_iota(jnp.int32, (sub, sub), 1)).astype(bf16)
    rank = jnp.dot(sel.astype(bf16), upper, preferred_element_type=f32)
    pos = base + rank
    eiota = lax.broadcasted_iota(jnp.int32, (ne, sub), 0).astype(f32)
    rem = sel
    pos_rows, w_rows = [], []
    for _ in range(TOP_K):
        ei = jnp.min(jnp.where(rem > 0.0, eiota, float(ne)), axis=0, keepdims=True)
        hit = eiota == ei
        pos_rows.append(jnp.sum(jnp.where(hit, pos, 0.0), axis=0, keepdims=True))
        w_rows.append(jnp.sum(jnp.where(hit, gates, 0.0), axis=0, keepdims=True))
        rem = jnp.where(hit, 0.0, rem)
    posl_ref[0] = jnp.concatenate(pos_rows, axis=0)
    gate_ref[0] = jnp.concatenate(w_rows, axis=0)


def _route(st, bias):
    ne, t = st.shape
    nt = t // SUB
    return pl.pallas_call(
        _route_kernel,
        grid=(nt,),
        in_specs=[pl.BlockSpec((ne, SUB), lambda i: (0, i)), pl.BlockSpec((ne, 1), lambda i: (0, 0))],
        out_specs=[pl.BlockSpec((ne, LANES), lambda i: (i, 0)),
                   pl.BlockSpec((1, TOP_K, SUB), lambda i: (i, 0, 0)),
                   pl.BlockSpec((1, TOP_K, SUB), lambda i: (i, 0, 0))],
        out_shape=[jax.ShapeDtypeStruct((nt * ne, LANES), f32),
                   jax.ShapeDtypeStruct((nt, TOP_K, SUB), f32),
                   jax.ShapeDtypeStruct((nt, TOP_K, SUB), f32)],
        compiler_params=_cparams("parallel"),
        name="route",
    )(st, bias)


def _rows_copy(loc_ref, hbm_ref, sem, slot, lo, go, rows, to_hbm):
    src = loc_ref.at[slot, pl.ds(lo, rows)]
    dst = hbm_ref.at[pl.ds(go, rows)]
    return pltpu.make_async_copy(src, dst, sem.at[slot]) if to_hbm else pltpu.make_async_copy(dst, src, sem.at[slot])


def _start_chunks(meta_ref, loc_ref, hbm_ref, sem, slot, to_hbm):
    def per_expert(e, base):
        c = pl.multiple_of(meta_ref[0, 0, e], ROW_ALIGN)
        g = pl.multiple_of(meta_ref[0, 1, e], ROW_ALIGN)
        _rows_copy(loc_ref, hbm_ref, sem, slot, pl.multiple_of(base, ROW_ALIGN), g, c, to_hbm).start()
        return base + c

    return lax.fori_loop(0, meta_ref.shape[2], per_expert, 0, unroll=8)


def _wait_rows(rows, loc_ref, hbm_ref, sem, slot, to_hbm):
    units = rows // ROW_ALIGN
    for bit in range((NLOC // ROW_ALIGN).bit_length()):
        @pl.when((units >> bit) & 1 == 1)
        def _():
            _rows_copy(loc_ref, hbm_ref, sem, slot, 0, 0, ROW_ALIGN << bit, to_hbm).wait()


def _select_chunk(posl, vals, r, sub):
    rel = posl - float(r * sub)
    rel = jnp.where((rel >= 0.0) & (rel < float(sub)), rel, -1.0).astype(bf16)
    piota = lax.broadcasted_iota(jnp.int32, (sub, sub), 0).astype(f32).astype(bf16)
    m = jnp.zeros((sub, sub), bf16)
    for k in range(TOP_K):
        val = jnp.ones((), bf16) if vals is None else vals[k:k + 1]
        m = jnp.where(piota == rel[k:k + 1], val, m)
    return m


def _dispatch_kernel(meta_ref, posl_ref, h_ref, xs_ref, loc_ref, sem, nrows):
    i = pl.program_id(0)
    last = pl.num_programs(0) - 1
    slot = i % 2
    sub = h_ref.shape[0]

    @pl.when(i >= 2)
    def _():
        _wait_rows(nrows[slot], loc_ref, xs_ref, sem, slot, True)

    def tile_rows(e, acc):
        return acc + meta_ref[0, 0, e]

    tot = lax.fori_loop(0, meta_ref.shape[2], tile_rows, 0, unroll=8)
    h = h_ref[...]
    posl = posl_ref[0]

    def sort_chunk(r):
        rows = jnp.dot(_select_chunk(posl, None, r, sub), h, preferred_element_type=f32)
        loc_ref[slot, r * sub:(r + 1) * sub, :] = rows.astype(loc_ref.dtype)

    always = (NLOC_MIN + NLOC) // (2 * sub)
    for r in range(always):
        sort_chunk(r)
    for r in range(always, NLOC // sub):
        pl.when(r * sub < tot)(functools.partial(sort_chunk, r))

    nrows[slot] = _start_chunks(meta_ref, loc_ref, xs_ref, sem, slot, True)

    @pl.when(i == last)
    def _():
        _wait_rows(nrows[slot], loc_ref, xs_ref, sem, slot, True)

        @pl.when(i >= 1)
        def _():
            _wait_rows(nrows[1 - slot], loc_ref, xs_ref, sem, 1 - slot, True)


def _dispatch(meta, posl, h2, rows_cap):
    t, d = h2.shape
    nt = t // SUB
    ne = meta.shape[2]
    return pl.pallas_call(
        _dispatch_kernel,
        grid=(nt,),
        in_specs=[pl.BlockSpec((1, 2, ne), lambda i: (i, 0, 0), memory_space=pltpu.SMEM),
                  pl.BlockSpec((1, TOP_K, SUB), lambda i: (i, 0, 0)),
                  pl.BlockSpec((SUB, d), lambda i: (i, 0))],
        out_specs=pl.BlockSpec(memory_space=pl.ANY),
        out_shape=jax.ShapeDtypeStruct((rows_cap, d), bf16),
        scratch_shapes=[pltpu.VMEM((2, NLOC, d), bf16), pltpu.SemaphoreType.DMA((2,)), pltpu.SMEM((2,), jnp.int32)],
        compiler_params=_cparams("arbitrary"),
        name="dispatch",
    )(meta, posl, h2)


def _gmm_kernel(be_ref, nv_ref, nb_ref, x_ref, wg_ref, wu_ref, wd_ref, y_ref):
    i = pl.program_id(0)
    nv = nv_ref[i]

    @pl.when(nv > 0)
    def _():
        x = x_ref[...]
        row = lax.broadcasted_iota(jnp.int32, x.shape, 0)
        x = jnp.where(row < nv, x, jnp.zeros((), x.dtype))
        a = _swiglu_hidden(x, wg_ref[0].astype(bf16), wu_ref[0].astype(bf16))
        y_ref[...] = jnp.dot(a.astype(bf16), wd_ref[0].astype(bf16), preferred_element_type=f32).astype(y_ref.dtype)


def _gmm(be, nv, nb, xs, weg, weu, wed):
    rows_cap, d = xs.shape
    de = weg.shape[2]
    blk = lambda i, be, nv, nb: (jnp.minimum(i, nb[0] - 1), 0)
    wsel = lambda i, be, nv, nb: (be[i], 0, 0)
    return pl.pallas_call(
        _gmm_kernel,
        grid_spec=pltpu.PrefetchScalarGridSpec(
            num_scalar_prefetch=3,
            grid=(rows_cap // GMM_BLOCK,),
            in_specs=[pl.BlockSpec((GMM_BLOCK, d), blk),
                      pl.BlockSpec((1, d, de), wsel), pl.BlockSpec((1, d, de), wsel), pl.BlockSpec((1, de, d), wsel)],
            out_specs=pl.BlockSpec((GMM_BLOCK, d), blk)),
        out_shape=jax.ShapeDtypeStruct((rows_cap, d), bf16),
        compiler_params=_cparams("arbitrary"),
        name="gmm",
    )(be, nv, nb, xs, weg, weu, wed)


def _combine_kernel(meta_ref, metan_ref, posl_ref, gate_ref, h_ref, wsg_ref, wsu_ref, wsd_ref, x1_ref, mod_ref, gp_ref,
                    ys_ref, o_ref, loc_ref, w_ref, sem, nrows):
    i = pl.program_id(0)
    last = pl.num_programs(0) - 1
    slot = i % 2
    sub = h_ref.shape[0]

    @pl.when(i == 0)
    def _():
        loc_ref[...] = jnp.zeros(loc_ref.shape, loc_ref.dtype)
        nrows[0] = _start_chunks(meta_ref, loc_ref, ys_ref, sem, 0, False)

    @pl.when(i < last)
    def _():
        nrows[1 - slot] = _start_chunks(metan_ref, loc_ref, ys_ref, sem, 1 - slot, False)

    _wait_rows(nrows[slot], loc_ref, ys_ref, sem, slot, False)

    h = h_ref[...]
    a = _swiglu_hidden(h, wsg_ref[...], wsu_ref[...])
    y = jnp.dot(a.astype(bf16), wsd_ref[...], preferred_element_type=f32)

    posl = posl_ref[0]
    gates = gate_ref[0].astype(bf16)
    for r in range(NLOC // sub):
        w_ref[r * sub:(r + 1) * sub, :] = _select_chunk(posl, gates, r, sub)
    y = y + lax.dot_general(w_ref[...], loc_ref[slot], TN_DIMS, preferred_element_type=f32)
    o_ref[...] = x1_ref[...] + mod_ref[0, 5:6, :] * (_rms(y) * gp_ref[...])


def _combine(meta, posl, gatel, h2, wsg, wsu, wsd, x1, mod, g_post, ys, *, s):
    t, d = h2.shape
    nt = t // SUB
    ne = meta.shape[2]
    ds = wsg.shape[1]
    per_b = s // SUB
    tile = lambda w: pl.BlockSpec((SUB, w), lambda i: (i, 0))
    const2 = lambda i: (0, 0)
    return pl.pallas_call(
        _combine_kernel,
        grid=(nt,),
        in_specs=[pl.BlockSpec((1, 2, ne), lambda i: (i, 0, 0), memory_space=pltpu.SMEM),
                  pl.BlockSpec((1, 2, ne), lambda i: (jnp.minimum(i + 1, nt - 1), 0, 0), memory_space=pltpu.SMEM),
                  pl.BlockSpec((1, TOP_K, SUB), lambda i: (i, 0, 0)), pl.BlockSpec((1, TOP_K, SUB), lambda i: (i, 0, 0)),
                  tile(d),
                  pl.BlockSpec((d, ds), const2), pl.BlockSpec((d, ds), const2), pl.BlockSpec((ds, d), const2),
                  tile(d), pl.BlockSpec((1, N_MOD, d), lambda i: (i // per_b, 0, 0)), pl.BlockSpec((1, d), const2),
                  pl.BlockSpec(memory_space=pl.ANY)],
        out_specs=tile(d),
        out_shape=jax.ShapeDtypeStruct((t, d), f32),
        scratch_shapes=[pltpu.VMEM((2, NLOC, d), bf16), pltpu.VMEM((NLOC, SUB), bf16),
                        pltpu.SemaphoreType.DMA((2,)), pltpu.SMEM((2,), jnp.int32)],
        compiler_params=_cparams("arbitrary"),
        name="combine",
    )(meta, meta, posl, gatel, h2, wsg, wsu, wsd, x1, mod, g_post, ys)


def _tiles(s, sc):
    tm = math.gcd(s, 512)
    assert sc % tm == 0 or tm % sc == 0
    return dict(tm=tm, tm_ctx=math.gcd(sc, tm), tq=math.gcd(s, 512))


def _dispatch_tables(cnt, nt, ne, rows_cap):
    n = cnt.reshape(nt, ne, LANES)[:, :, 0].astype(jnp.int32)
    c = jnp.maximum((n + (ROW_ALIGN - 1)) // ROW_ALIGN, 1) * ROW_ALIGN
    n_e = jnp.sum(c, axis=0)
    p_e = (n_e + (GMM_BLOCK - 1)) // GMM_BLOCK * GMM_BLOCK
    seg_end = jnp.cumsum(p_e)
    seg_start = seg_end - p_e
    g = seg_start[None, :] + jnp.cumsum(c, axis=0) - c
    meta = jnp.stack([c, g], axis=1)
    blk_start = jnp.arange(rows_cap // GMM_BLOCK, dtype=jnp.int32) * GMM_BLOCK
    be = jnp.minimum(jnp.sum(blk_start[:, None] >= seg_end[None, :], axis=1), ne - 1).astype(jnp.int32)
    nv = jnp.clip(n_e[be] - (blk_start - seg_start[be]), 0, GMM_BLOCK).astype(jnp.int32)
    nb = jnp.maximum(seg_end[-1:] // GMM_BLOCK, 1).astype(jnp.int32)
    return meta, be, nv, nb


def _rope_tables(s):
    n_freq = HEAD_DIM // 4
    inv_freq = ROPE_THETA ** (-jnp.arange(n_freq, dtype=f32) / n_freq)
    pos = jnp.arange(s)
    ang_r = (pos // GRID_W).astype(f32)[:, None] * inv_freq
    ang_c = (pos % GRID_W).astype(f32)[:, None] * inv_freq
    cr, sr, cc, sc_ = jnp.cos(ang_r), jnp.sin(ang_r), jnp.cos(ang_c), jnp.sin(ang_c)
    cos = jnp.concatenate([cr, cr, cc, cc] * (LANES // HEAD_DIM), axis=1)
    sin = jnp.concatenate([-sr, sr, -sc_, sc_] * (LANES // HEAD_DIM), axis=1)
    return cos, sin


def kernel(x, c, ctx, c_ctx, w_mod, b_mod, g_pre_mix, g_post_mix, w_in, q_norm_g, k_norm_g, lambda_q1, lambda_k1,
           lambda_q2, lambda_k2, subln_g, w_out, g_pre_ffn, g_post_ffn, w_router, router_bias, w_exp_gate, w_exp_up,
           w_exp_down, w_sh_gate, w_sh_up, w_sh_down):
    assert w_mod.shape[0] == 1, "single-layer block"
    b, s, d = x.shape
    sc = ctx.shape[1]
    sk = s + sc
    ne = w_router.shape[2]
    assert w_in.shape[2] == IN_WIDTH and ne % N_GROUPS == 0 and ne <= LANES
    tl = _tiles(s, sc)

    pad = (-(b + 1)) % 8
    cc = jnp.concatenate([c, c_ctx[None], jnp.zeros((pad, d), f32)], axis=0)
    mod_all = _mod(cc, w_mod[0], b_mod[0])
    mod = mod_all[:b].reshape(b, N_MOD, d)
    mod_c = mod_all[b:b + 1].reshape(1, N_MOD, d)

    w_in_b = w_in[0].astype(bf16)
    qg = jnp.tile(q_norm_g[0], GQA_HEADS)[None]
    kg = jnp.tile(k_norm_g[0], GQA_KV_HEADS)[None]
    head_id = jnp.arange(GQ_W) // HEAD_DIM
    bd = (head_id[:, None] == head_id[None, :]).astype(bf16)
    cos, sin = _rope_tables(s)
    g_pre = g_pre_mix[0][None]

    dq, gq, dk, dv, gk, gv = _inproj(x, mod, g_pre, w_in_b, qg, kg, bd, cos, sin, None,
                                     latent=True, sk=sk, row_off=0, tm=tl["tm"])
    dk, dv, gk, gv = _inproj(ctx, mod_c, g_pre, w_in_b, qg, kg, bd, cos, sin, (dk, dv, gk, gv),
                             latent=False, sk=sk, row_off=s, tm=tl["tm_ctx"])

    lamv = jnp.stack([lambda_q1[0], lambda_k1[0], lambda_q2[0], lambda_k2[0]])
    mixed = _attention(dq, gq, dk, dv, gk, gv, lamv, subln_g[0][None], tq=tl["tq"])

    x1, h2, st = _outproj(mixed, x, mod, w_out[0].astype(bf16), g_post_mix[0][None], g_pre_ffn[0][None],
                          w_router[0].T.astype(bf16), tm=tl["tm"])
    t = b * s
    nt = t // SUB
    assert s % SUB == 0 and ne <= N_EXPERTS_MAX
    cnt, posl, gatel = _route(st, router_bias[0][:, None])
    rows_cap = -(-(t * TOP_K + nt * ne * ROW_ALIGN + ne * (GMM_BLOCK - 1)) // GMM_BLOCK) * GMM_BLOCK
    meta, be, nv, nb = _dispatch_tables(cnt, nt, ne, rows_cap)
    h2f = h2.reshape(t, d)
    xs = _dispatch(meta, posl, h2f, rows_cap)
    ys = _gmm(be, nv, nb, xs, w_exp_gate[0], w_exp_up[0], w_exp_down[0])
    out = _combine(meta, posl, gatel, h2f, w_sh_gate[0].astype(bf16), w_sh_up[0].astype(bf16), w_sh_down[0].astype(bf16),
                   x1.reshape(t, d), mod, g_post_ffn[0][None], ys, s=s)
    return out.reshape(b, s, d)
```

```python
import functools
import math

import jax
import jax.numpy as jnp
from jax import lax
from jax.experimental import pallas as pl
from jax.experimental.pallas import tpu as pltpu

HEAD_DIM = 64
DIFF_HEADS = 4
GQA_HEADS = 8
GQA_KV_HEADS = 2
GRID_W = 64
ROPE_THETA = 10000.0
N_GROUPS = 8
TOPK_GROUPS = 4
TOP_K = 8
ROUTED_SCALE = 2.5
N_MOD = 6
EPS = 1e-6
LAM_INIT = 0.8 - 0.6 * math.exp(-0.3 * 0)

LANES = 128
DQ_W = DIFF_HEADS * 2 * HEAD_DIM
DV_W = DIFF_HEADS * 2 * HEAD_DIM
GQ_W = GQA_HEADS * HEAD_DIM
GK_W = GQA_KV_HEADS * HEAD_DIM
OFF_DQ, OFF_DK, OFF_DV = 0, DQ_W, 2 * DQ_W
OFF_GQ = OFF_DV + DV_W
OFF_GK = OFF_GQ + GQ_W
OFF_GV = OFF_GK + GK_W
IN_WIDTH = OFF_GV + GK_W

VMEM_LIMIT = 56 * 1024 * 1024

N_EXPERTS_MAX = 64
SUB = 256
ROW_ALIGN = 16
GMM_BLOCK = 2048
NLOC = -(-(SUB * TOP_K + N_EXPERTS_MAX * ROW_ALIGN) // SUB) * SUB
NLOC_MIN = SUB * TOP_K // SUB * SUB


def _swiglu_hidden(h, wg, wu):
    return jax.nn.silu(jnp.dot(h, wg, preferred_element_type=f32)) * jnp.dot(h, wu, preferred_element_type=f32)

f32 = jnp.float32
bf16 = jnp.bfloat16
NT_DIMS = (((1,), (1,)), ((), ()))
TN_DIMS = (((0,), (0,)), ((), ()))


def _cparams(*sem):
    return pltpu.CompilerParams(dimension_semantics=sem, vmem_limit_bytes=VMEM_LIMIT)


def _rms(x):
    return x * lax.rsqrt(jnp.mean(x * x, axis=-1, keepdims=True) + EPS)


def _mod_kernel(c_ref, w_ref, b_ref, o_ref):
    o_ref[...] = jnp.dot(jax.nn.silu(c_ref[...]), w_ref[...], preferred_element_type=f32) + b_ref[...]


def _mod(cc, w_mod, b_mod):
    rows, d = cc.shape
    n = w_mod.shape[1]
    bn = d
    return pl.pallas_call(
        _mod_kernel,
        grid=(n // bn,),
        in_specs=[pl.BlockSpec((rows, d), lambda j: (0, 0)),
                  pl.BlockSpec((d, bn), lambda j: (0, j)),
                  pl.BlockSpec((1, bn), lambda j: (0, j))],
        out_specs=pl.BlockSpec((rows, bn), lambda j: (0, j)),
        out_shape=jax.ShapeDtypeStruct((rows, n), f32),
        compiler_params=_cparams("arbitrary"),
        name="mod",
    )(cc, w_mod, b_mod.reshape(1, n))


def _rope_tile(xt, cos, sin_signed, lane_lo):
    partner = jnp.where(lane_lo, pltpu.roll(xt, LANES - 16, 1), pltpu.roll(xt, 16, 1))
    return xt * cos + partner * sin_signed


def _inproj_kernel(x_ref, mod_ref, g_ref, w_ref, qg_ref, kg_ref, bd_ref, cos_ref, sin_ref, *out_refs,
                   latent):
    x = x_ref[0]
    h = _rms(x) * g_ref[...]
    h = h * (1.0 + mod_ref[0, 1:2, :]) + mod_ref[0, 0:1, :]
    hb = h.astype(bf16)
    tm = x.shape[0]

    def proj(off, width):
        return jnp.dot(hb, w_ref[:, off:off + width], preferred_element_type=f32)

    def head_rms(v, gain):
        w = v.shape[1]
        ss = jnp.dot((v * v).astype(bf16), bd_ref[0:w, 0:w], preferred_element_type=f32)
        return v * lax.rsqrt(ss * (1.0 / HEAD_DIM) + EPS) * gain

    if latent:
        dq_ref, gq_ref, dk_ref, dv_ref, gk_ref, gv_ref = out_refs
        cos, sin = cos_ref[...], sin_ref[...]
        lane_lo = (lax.broadcasted_iota(jnp.int32, (tm, LANES), 1) % 32) < 16
        rope = lambda v: _rope_tile(v, cos, sin, lane_lo)
    else:
        dk_ref, dv_ref, gk_ref, gv_ref = out_refs[-4:]
        rope = lambda v: v

    def store(ref, v, fn):
        for t in range(v.shape[1] // LANES):
            sl = slice(LANES * t, LANES * (t + 1))
            ref[0, :, sl] = fn(v[:, sl]).astype(ref.dtype)

    qscale = HEAD_DIM ** -0.5 * math.log2(math.e)
    if latent:
        store(dq_ref, proj(OFF_DQ, DQ_W) * qscale, rope)
        store(gq_ref, head_rms(proj(OFF_GQ, GQ_W), qg_ref[...] * qscale), rope)
    store(dk_ref, proj(OFF_DK, DQ_W), rope)
    store(gk_ref, head_rms(proj(OFF_GK, GK_W), kg_ref[...]), rope)
    dv_ref[0] = proj(OFF_DV, DV_W).astype(bf16)
    gv_ref[0] = proj(OFF_GV, GK_W).astype(bf16)


def _inproj(x, mod, g_pre, w_in_b, qg, kg, bd, cos, sin, kv_bufs, *, latent, sk, row_off, tm):
    b, s, d = x.shape
    nt = s // tm
    kv_blk = row_off // tm
    kv_widths = (DQ_W, DV_W, GK_W, GK_W)
    const2 = lambda bi, i: (0, 0)
    in_specs = [
        pl.BlockSpec((1, tm, d), lambda bi, i: (bi, i, 0)),
        pl.BlockSpec((1, N_MOD, d), (lambda bi, i: (bi, 0, 0)) if latent else (lambda bi, i: (0, 0, 0))),
        pl.BlockSpec((1, d), const2),
        pl.BlockSpec((d, IN_WIDTH), const2),
        pl.BlockSpec((1, GQ_W), const2),
        pl.BlockSpec((1, GK_W), const2),
        pl.BlockSpec((GQ_W, GQ_W), const2),
        pl.BlockSpec((tm, LANES), (lambda bi, i: (i, 0)) if latent else const2),
        pl.BlockSpec((tm, LANES), (lambda bi, i: (i, 0)) if latent else const2),
    ]
    kv_specs = [pl.BlockSpec((1, tm, w), lambda bi, i: (bi, kv_blk + i, 0)) for w in kv_widths]
    kv_shapes = [jax.ShapeDtypeStruct((b, sk, w), bf16) for w in kv_widths]
    args = [x, mod, g_pre, w_in_b, qg, kg, bd, cos, sin]
    if latent:
        out_specs = [pl.BlockSpec((1, tm, DQ_W), lambda bi, i: (bi, i, 0)),
                     pl.BlockSpec((1, tm, GQ_W), lambda bi, i: (bi, i, 0))] + kv_specs
        out_shape = [jax.ShapeDtypeStruct((b, s, DQ_W), bf16), jax.ShapeDtypeStruct((b, s, GQ_W), bf16)] + kv_shapes
        aliases = {}
    else:
        in_specs += [pl.BlockSpec(memory_space=pl.ANY)] * 4
        args += list(kv_bufs)
        out_specs, out_shape = kv_specs, kv_shapes
        aliases = {len(args) - 4 + k: k for k in range(4)}
    return pl.pallas_call(
        functools.partial(_inproj_kernel, latent=latent),
        grid=(b, nt),
        in_specs=in_specs,
        out_specs=out_specs,
        out_shape=out_shape,
        input_output_aliases=aliases,
        compiler_params=_cparams("parallel", "arbitrary"),
        name="inproj_latent" if latent else "inproj_ctx",
    )(*args)


def _attn_kernel(dq_ref, gq_ref, dk_ref, dv_ref, gk_ref, gv_ref, lam_ref, sg_ref, o_ref, vext_ref, s_ref, p_ref):
    tq = dq_ref.shape[1]
    sk = dk_ref.shape[1]

    @pl.when(pl.program_id(1) == 0)
    def _():
        ones = jnp.ones((sk, LANES), bf16)
        for h in range(DIFF_HEADS):
            vext_ref[h, :, :LANES] = dv_ref[0, :, LANES * h:LANES * (h + 1)]
            vext_ref[h, :, LANES:] = ones
        vext_ref[DIFF_HEADS, :, :LANES] = gv_ref[0]
        vext_ref[DIFF_HEADS, :, LANES:] = ones

    lane = lax.broadcasted_iota(jnp.int32, (tq, LANES), 1)
    lo = lane < HEAD_DIM
    lv = lam_ref[...]
    lam = (jnp.exp(jnp.sum(lv[0:1] * lv[1:2], axis=1, keepdims=True))
           - jnp.exp(jnp.sum(lv[2:3] * lv[3:4], axis=1, keepdims=True)) + LAM_INIT)
    zero = jnp.zeros((), bf16)
    group = GQA_HEADS // GQA_KV_HEADS

    maps = [("diff", h, w) for h in range(DIFF_HEADS) for w in range(2)]
    maps += [("gqa", t, pos) for t in range(GQA_HEADS // 2) for pos in range(2)]

    def scores(kind, a, b):
        if kind == "diff":
            sl = slice(LANES * a, LANES * (a + 1))
            qt = dq_ref[0, :, sl]
            qm = jnp.where(lo, qt, zero) if b == 0 else jnp.where(lo, zero, qt)
            return lax.dot_general(qm, dk_ref[0, :, sl], NT_DIMS, preferred_element_type=f32)
        j = (2 * a + b) // group
        qt = gq_ref[0, :, LANES * a:LANES * (a + 1)]
        if b != j:
            qt = pltpu.roll(qt, HEAD_DIM, 1)
        qm = jnp.where(lo if j == 0 else ~lo, qt, zero)
        return lax.dot_general(qm, gk_ref[0], NT_DIMS, preferred_element_type=f32)

    def values(i):
        kind, a, b = maps[i]
        r = jnp.dot(p_ref[i % 2], vext_ref[a if kind == "diff" else DIFF_HEADS], preferred_element_type=f32)
        o = r[:, :LANES] / r[:, LANES:]
        if kind == "gqa" and b != (2 * a + b) // group:
            o = pltpu.roll(o, HEAD_DIM, 1)
        return o

    outs = []
    s_ref[0] = scores(*maps[0])
    for i in range(len(maps)):
        if i + 1 < len(maps):
            s_ref[(i + 1) % 2] = scores(*maps[i + 1])
        s = s_ref[i % 2]
        p_ref[i % 2] = jnp.exp2(s - jnp.max(s, axis=-1, keepdims=True)).astype(bf16)
        outs.append(values(i))

    for h in range(DIFF_HEADS):
        o = _rms(outs[2 * h] - lam * outs[2 * h + 1]) * (sg_ref[...] * (1.0 - LAM_INIT))
        o_ref[0, :, LANES * h:LANES * (h + 1)] = o.astype(o_ref.dtype)
    for t in range(GQA_HEADS // 2):
        base = 2 * DIFF_HEADS + 2 * t
        off = DIFF_HEADS * 2 * HEAD_DIM + LANES * t
        o_ref[0, :, off:off + LANES] = jnp.where(lo, outs[base], outs[base + 1]).astype(o_ref.dtype)


def _attention(dq, gq, dk, dv, gk, gv, lamv, subln_g, *, tq):
    b, s, _ = dq.shape
    sk = dk.shape[1]
    qspec = lambda w: pl.BlockSpec((1, tq, w), lambda bi, qi: (bi, qi, 0))
    kspec = lambda w: pl.BlockSpec((1, sk, w), lambda bi, qi: (bi, 0, 0))
    const2 = lambda bi, qi: (0, 0)
    mix_w = DV_W + GQ_W
    return pl.pallas_call(
        _attn_kernel,
        grid=(b, s // tq),
        in_specs=[qspec(DQ_W), qspec(GQ_W), kspec(DQ_W), kspec(DV_W), kspec(GK_W), kspec(GK_W),
                  pl.BlockSpec((4, HEAD_DIM), const2), pl.BlockSpec((1, 2 * HEAD_DIM), const2)],
        out_specs=pl.BlockSpec((1, tq, mix_w), lambda bi, qi: (bi, qi, 0)),
        out_shape=jax.ShapeDtypeStruct((b, s, mix_w), bf16),
        scratch_shapes=[pltpu.VMEM((DIFF_HEADS + 1, sk, 2 * LANES), bf16),
                        pltpu.VMEM((2, tq, sk), f32), pltpu.VMEM((2, tq, sk), bf16)],
        compiler_params=_cparams("parallel", "arbitrary"),
        name="attn",
    )(dq, gq, dk, dv, gk, gv, lamv, subln_g)


def _outproj_kernel(mix_ref, x_ref, mod_ref, wo_ref, gpost_ref, gpre_ref, wr_ref, x1_ref, h2_ref, st_ref):
    o = jnp.dot(mix_ref[0], wo_ref[...], preferred_element_type=f32)
    x1 = x_ref[0] + mod_ref[0, 2:3, :] * (_rms(o) * gpost_ref[...])
    x1_ref[0] = x1
    h2 = _rms(x1) * gpre_ref[...]
    h2 = (h2 * (1.0 + mod_ref[0, 4:5, :]) + mod_ref[0, 3:4, :]).astype(bf16)
    h2_ref[0] = h2
    st_ref[...] = jax.nn.sigmoid(lax.dot_general(wr_ref[...], h2, NT_DIMS, preferred_element_type=f32))


def _outproj(mixed, x, mod, w_out_b, g_post, g_pre, wr_t, *, tm):
    b, s, d = x.shape
    nt = s // tm
    ne = wr_t.shape[0]
    mw = mixed.shape[2]
    const2 = lambda bi, i: (0, 0)
    tile = lambda w: pl.BlockSpec((1, tm, w), lambda bi, i: (bi, i, 0))
    return pl.pallas_call(
        _outproj_kernel,
        grid=(b, nt),
        in_specs=[tile(mw), tile(d), pl.BlockSpec((1, N_MOD, d), lambda bi, i: (bi, 0, 0)),
                  pl.BlockSpec((mw, d), const2), pl.BlockSpec((1, d), const2), pl.BlockSpec((1, d), const2),
                  pl.BlockSpec((ne, d), const2)],
        out_specs=[tile(d), tile(d), pl.BlockSpec((ne, tm), lambda bi, i: (0, bi * nt + i))],
        out_shape=[jax.ShapeDtypeStruct((b, s, d), f32), jax.ShapeDtypeStruct((b, s, d), bf16),
                   jax.ShapeDtypeStruct((ne, b * s), f32)],
        compiler_params=_cparams("parallel", "arbitrary"),
        name="outproj",
    )(mixed, x, mod, w_out_b, g_post, g_pre, wr_t)


def _first_argmax(v, iota, n):
    m = jnp.max(v, axis=0, keepdims=True)
    idx = jnp.min(jnp.where(v == m, iota, n), axis=0, keepdims=True)
    return m, idx


def _route_gates_t(s, bias):
    ne, tm = s.shape
    gsz = ne // N_GROUPS
    neg = -jnp.inf
    ssel = s + bias
    sub = lax.broadcasted_iota(jnp.int32, (gsz, tm), 0).astype(f32)
    gscore = []
    for g in range(N_GROUPS):
        blk = ssel[gsz * g:gsz * (g + 1)]
        m1, i1 = _first_argmax(blk, sub, float(gsz))
        m2 = jnp.max(jnp.where(sub == i1, neg, blk), axis=0, keepdims=True)
        gscore.append(m1 + m2)
    gs = jnp.concatenate(gscore, axis=0)
    giota = lax.broadcasted_iota(jnp.int32, (N_GROUPS, tm), 0).astype(f32)
    gsel = jnp.zeros((N_GROUPS, tm), f32)
    for _ in range(TOPK_GROUPS):
        _, gi = _first_argmax(gs, giota, float(N_GROUPS))
        hit = giota == gi
        gsel = jnp.where(hit, 1.0, gsel)
        gs = jnp.where(hit, neg, gs)
    emask = jnp.concatenate([jnp.broadcast_to(gsel[g:g + 1], (gsz, tm)) for g in range(N_GROUPS)], axis=0)
    cand = jnp.where(emask > 0.0, ssel, neg)
    eiota = lax.broadcasted_iota(jnp.int32, (ne, tm), 0).astype(f32)
    w = jnp.zeros((ne, tm), f32)
    sel = jnp.zeros((ne, tm), f32)
    for _ in range(TOP_K):
        _, ei = _first_argmax(cand, eiota, float(ne))
        hit = eiota == ei
        w = jnp.where(hit, s, w)
        sel = jnp.where(hit, 1.0, sel)
        cand = jnp.where(hit, neg, cand)
    return w / jnp.sum(w, axis=0, keepdims=True) * ROUTED_SCALE, sel


def _route_kernel(st_ref, bias_ref, cnt_ref, posl_ref, gate_ref):
    gates, sel = _route_gates_t(st_ref[...], bias_ref[...])
    ne, sub = sel.shape
    n = jnp.sum(sel, axis=1, keepdims=True)
    cnt_ref[...] = jnp.broadcast_to(n, (ne, LANES))
    c = jnp.maximum(jnp.floor((n + (ROW_ALIGN - 1)) * (1.0 / ROW_ALIGN)), 1.0) * ROW_ALIGN
    lower = (lax.broadcasted_iota(jnp.int32, (ne, ne), 1) < lax.broadcasted_iota(jnp.int32, (ne, ne), 0)).astype(f32)
    base = jnp.dot(lower, jnp.broadcast_to(c, (ne, LANES)), preferred_element_type=f32)[:, 0:1]
    upper = (lax.broadcasted_iota(jnp.int32, (sub, sub), 0) < lax.broadcasted_iota(jnp.int32, (sub, sub), 1)).astype(bf16)
    rank = jnp.dot(sel.astype(bf16), upper, preferred_element_type=f32)
    pos = base + rank
    eiota = lax.broadcasted_iota(jnp.int32, (ne, sub), 0).astype(f32)
    rem = sel
    pos_rows, w_rows = [], []
    for _ in range(TOP_K):
        ei = jnp.min(jnp.where(rem > 0.0, eiota, float(ne)), axis=0, keepdims=True)
        hit = eiota == ei
        pos_rows.append(jnp.sum(jnp.where(hit, pos, 0.0), axis=0, keepdims=True))
        w_rows.append(jnp.sum(jnp.where(hit, gates, 0.0), axis=0, keepdims=True))
        rem = jnp.where(hit, 0.0, rem)
    posl_ref[0] = jnp.concatenate(pos_rows, axis=0)
    gate_ref[0] = jnp.concatenate(w_rows, axis=0)


def _route(st, bias):
    ne, t = st.shape
    nt = t // SUB
    return pl.pallas_call(
        _route_kernel,
        grid=(nt,),
        in_specs=[pl.BlockSpec((ne, SUB), lambda i: (0, i)), pl.BlockSpec((ne, 1), lambda i: (0, 0))],
        out_specs=[pl.BlockSpec((ne, LANES), lambda i: (i, 0)),
                   pl.BlockSpec((1, TOP_K, SUB), lambda i: (i, 0, 0)),
                   pl.BlockSpec((1, TOP_K, SUB), lambda i: (i, 0, 0))],
        out_shape=[jax.ShapeDtypeStruct((nt * ne, LANES), f32),
                   jax.ShapeDtypeStruct((nt, TOP_K, SUB), f32),
                   jax.ShapeDtypeStruct((nt, TOP_K, SUB), f32)],
        compiler_params=_cparams("parallel"),
        name="route",
    )(st, bias)


def _rows_copy(loc_ref, hbm_ref, sem, slot, lo, go, rows, to_hbm):
    src = loc_ref.at[slot, pl.ds(lo, rows)]
    dst = hbm_ref.at[pl.ds(go, rows)]
    return pltpu.make_async_copy(src, dst, sem.at[slot]) if to_hbm else pltpu.make_async_copy(dst, src, sem.at[slot])


def _start_chunks(meta_ref, loc_ref, hbm_ref, sem, slot, to_hbm):
    def per_expert(e, base):
        c = pl.multiple_of(meta_ref[0, 0, e], ROW_ALIGN)
        g = pl.multiple_of(meta_ref[0, 1, e], ROW_ALIGN)
        _rows_copy(loc_ref, hbm_ref, sem, slot, pl.multiple_of(base, ROW_ALIGN), g, c, to_hbm).start()
        return base + c

    return lax.fori_loop(0, meta_ref.shape[2], per_expert, 0, unroll=8)


def _wait_rows(rows, loc_ref, hbm_ref, sem, slot, to_hbm):
    units = rows // ROW_ALIGN
    for bit in range((NLOC // ROW_ALIGN).bit_length()):
        @pl.when((units >> bit) & 1 == 1)
        def _():
            _rows_copy(loc_ref, hbm_ref, sem, slot, 0, 0, ROW_ALIGN << bit, to_hbm).wait()


def _select_chunk(posl, vals, r, sub):
    rel = posl - float(r * sub)
    rel = jnp.where((rel >= 0.0) & (rel < float(sub)), rel, -1.0).astype(bf16)
    piota = lax.broadcasted_iota(jnp.int32, (sub, sub), 0).astype(f32).astype(bf16)
    m = jnp.zeros((sub, sub), bf16)
    for k in range(TOP_K):
        val = jnp.ones((), bf16) if vals is None else vals[k:k + 1]
        m = jnp.where(piota == rel[k:k + 1], val, m)
    return m


def _dispatch_kernel(meta_ref, posl_ref, h_ref, xs_ref, loc_ref, sem, nrows):
    i = pl.program_id(0)
    last = pl.num_programs(0) - 1
    slot = i % 2
    sub = h_ref.shape[0]

    @pl.when(i >= 2)
    def _():
        _wait_rows(nrows[slot], loc_ref, xs_ref, sem, slot, True)

    def tile_rows(e, acc):
        return acc + meta_ref[0, 0, e]

    tot = lax.fori_loop(0, meta_ref.shape[2], tile_rows, 0, unroll=8)
    h = h_ref[...]
    posl = posl_ref[0]

    def sort_chunk(r):
        rows = jnp.dot(_select_chunk(posl, None, r, sub), h, preferred_element_type=f32)
        loc_ref[slot, r * sub:(r + 1) * sub, :] = rows.astype(loc_ref.dtype)

    always = (NLOC_MIN + NLOC) // (2 * sub)
    for r in range(always):
        sort_chunk(r)
    for r in range(always, NLOC // sub):
        pl.when(r * sub < tot)(functools.partial(sort_chunk, r))

    nrows[slot] = _start_chunks(meta_ref, loc_ref, xs_ref, sem, slot, True)

    @pl.when(i == last)
    def _():
        _wait_rows(nrows[slot], loc_ref, xs_ref, sem, slot, True)

        @pl.when(i >= 1)
        def _():
            _wait_rows(nrows[1 - slot], loc_ref, xs_ref, sem, 1 - slot, True)


def _dispatch(meta, posl, h2, rows_cap):
    t, d = h2.shape
    nt = t // SUB
    ne = meta.shape[2]
    return pl.pallas_call(
        _dispatch_kernel,
        grid=(nt,),
        in_specs=[pl.BlockSpec((1, 2, ne), lambda i: (i, 0, 0), memory_space=pltpu.SMEM),
                  pl.BlockSpec((1, TOP_K, SUB), lambda i: (i, 0, 0)),
                  pl.BlockSpec((SUB, d), lambda i: (i, 0))],
        out_specs=pl.BlockSpec(memory_space=pl.ANY),
        out_shape=jax.ShapeDtypeStruct((rows_cap, d), bf16),
        scratch_shapes=[pltpu.VMEM((2, NLOC, d), bf16), pltpu.SemaphoreType.DMA((2,)), pltpu.SMEM((2,), jnp.int32)],
        compiler_params=_cparams("arbitrary"),
        name="dispatch",
    )(meta, posl, h2)


def _gmm_kernel(be_ref, nv_ref, nb_ref, x_ref, wg_ref, wu_ref, wd_ref, y_ref):
    i = pl.program_id(0)
    nv = nv_ref[i]

    @pl.when(nv > 0)
    def _():
        x = x_ref[...]
        row = lax.broadcasted_iota(jnp.int32, x.shape, 0)
        x = jnp.where(row < nv, x, jnp.zeros((), x.dtype))
        a = _swiglu_hidden(x, wg_ref[0].astype(bf16), wu_ref[0].astype(bf16))
        y_ref[...] = jnp.dot(a.astype(bf16), wd_ref[0].astype(bf16), preferred_element_type=f32).astype(y_ref.dtype)


def _gmm(be, nv, nb, xs, weg, weu, wed):
    rows_cap, d = xs.shape
    de = weg.shape[2]
    blk = lambda i, be, nv, nb: (jnp.minimum(i, nb[0] - 1), 0)
    wsel = lambda i, be, nv, nb: (be[i], 0, 0)
    return pl.pallas_call(
        _gmm_kernel,
        grid_spec=pltpu.PrefetchScalarGridSpec(
            num_scalar_prefetch=3,
            grid=(rows_cap // GMM_BLOCK,),
            in_specs=[pl.BlockSpec((GMM_BLOCK, d), blk),
                      pl.BlockSpec((1, d, de), wsel), pl.BlockSpec((1, d, de), wsel), pl.BlockSpec((1, de, d), wsel)],
            out_specs=pl.BlockSpec((GMM_BLOCK, d), blk)),
        out_shape=jax.ShapeDtypeStruct((rows_cap, d), bf16),
        compiler_params=_cparams("arbitrary"),
        name="gmm",
    )(be, nv, nb, xs, weg, weu, wed)


def _combine_kernel(meta_ref, metan_ref, posl_ref, gate_ref, h_ref, wsg_ref, wsu_ref, wsd_ref, x1_ref, mod_ref, gp_ref,
                    ys_ref, o_ref, loc_ref, w_ref, sem, nrows):
    i = pl.program_id(0)
    last = pl.num_programs(0) - 1
    slot = i % 2
    sub = h_ref.shape[0]

    @pl.when(i == 0)
    def _():
        loc_ref[...] = jnp.zeros(loc_ref.shape, loc_ref.dtype)
        nrows[0] = _start_chunks(meta_ref, loc_ref, ys_ref, sem, 0, False)

    @pl.when(i < last)
    def _():
        nrows[1 - slot] = _start_chunks(metan_ref, loc_ref, ys_ref, sem, 1 - slot, False)

    _wait_rows(nrows[slot], loc_ref, ys_ref, sem, slot, False)

    h = h_ref[...]
    a = _swiglu_hidden(h, wsg_ref[...], wsu_ref[...])
    y = jnp.dot(a.astype(bf16), wsd_ref[...], preferred_element_type=f32)

    posl = posl_ref[0]
    gates = gate_ref[0].astype(bf16)
    for r in range(NLOC // sub):
        w_ref[r * sub:(r + 1) * sub, :] = _select_chunk(posl, gates, r, sub)
    y = y + lax.dot_general(w_ref[...], loc_ref[slot], TN_DIMS, preferred_element_type=f32)
    o_ref[...] = x1_ref[...] + mod_ref[0, 5:6, :] * (_rms(y) * gp_ref[...])


def _combine(meta, posl, gatel, h2, wsg, wsu, wsd, x1, mod, g_post, ys, *, s):
    t, d = h2.shape
    nt = t // SUB
    ne = meta.shape[2]
    ds = wsg.shape[1]
    per_b = s // SUB
    tile = lambda w: pl.BlockSpec((SUB, w), lambda i: (i, 0))
    const2 = lambda i: (0, 0)
    return pl.pallas_call(
        _combine_kernel,
        grid=(nt,),
        in_specs=[pl.BlockSpec((1, 2, ne), lambda i: (i, 0, 0), memory_space=pltpu.SMEM),
                  pl.BlockSpec((1, 2, ne), lambda i: (jnp.minimum(i + 1, nt - 1), 0, 0), memory_space=pltpu.SMEM),
                  pl.BlockSpec((1, TOP_K, SUB), lambda i: (i, 0, 0)), pl.BlockSpec((1, TOP_K, SUB), lambda i: (i, 0, 0)),
                  tile(d),
                  pl.BlockSpec((d, ds), const2), pl.BlockSpec((d, ds), const2), pl.BlockSpec((ds, d), const2),
                  tile(d), pl.BlockSpec((1, N_MOD, d), lambda i: (i // per_b, 0, 0)), pl.BlockSpec((1, d), const2),
                  pl.BlockSpec(memory_space=pl.ANY)],
        out_specs=tile(d),
        out_shape=jax.ShapeDtypeStruct((t, d), f32),
        scratch_shapes=[pltpu.VMEM((2, NLOC, d), bf16), pltpu.VMEM((NLOC, SUB), bf16),
                        pltpu.SemaphoreType.DMA((2,)), pltpu.SMEM((2,), jnp.int32)],
        compiler_params=_cparams("arbitrary"),
        name="combine",
    )(meta, meta, posl, gatel, h2, wsg, wsu, wsd, x1, mod, g_post, ys)


def _tiles(s, sc):
    tm = math.gcd(s, 1024)
    assert sc % tm == 0 or tm % sc == 0
    return dict(tm=tm, tm_ctx=math.gcd(sc, tm), tq=math.gcd(s, 512))


def _dispatch_tables(cnt, nt, ne, rows_cap):
    n = cnt.reshape(nt, ne, LANES)[:, :, 0].astype(jnp.int32)
    c = jnp.maximum((n + (ROW_ALIGN - 1)) // ROW_ALIGN, 1) * ROW_ALIGN
    n_e = jnp.sum(c, axis=0)
    p_e = (n_e + (GMM_BLOCK - 1)) // GMM_BLOCK * GMM_BLOCK
    seg_end = jnp.cumsum(p_e)
    seg_start = seg_end - p_e
    g = seg_start[None, :] + jnp.cumsum(c, axis=0) - c
    meta = jnp.stack([c, g], axis=1)
    blk_start = jnp.arange(rows_cap // GMM_BLOCK, dtype=jnp.int32) * GMM_BLOCK
    be = jnp.minimum(jnp.sum(blk_start[:, None] >= seg_end[None, :], axis=1), ne - 1).astype(jnp.int32)
    nv = jnp.clip(n_e[be] - (blk_start - seg_start[be]), 0, GMM_BLOCK).astype(jnp.int32)
    nb = jnp.maximum(seg_end[-1:] // GMM_BLOCK, 1).astype(jnp.int32)
    return meta, be, nv, nb


def _rope_tables(s):
    n_freq = HEAD_DIM // 4
    inv_freq = ROPE_THETA ** (-jnp.arange(n_freq, dtype=f32) / n_freq)
    pos = jnp.arange(s)
    ang_r = (pos // GRID_W).astype(f32)[:, None] * inv_freq
    ang_c = (pos % GRID_W).astype(f32)[:, None] * inv_freq
    cr, sr, cc, sc_ = jnp.cos(ang_r), jnp.sin(ang_r), jnp.cos(ang_c), jnp.sin(ang_c)
    cos = jnp.concatenate([cr, cr, cc, cc] * (LANES // HEAD_DIM), axis=1)
    sin = jnp.concatenate([-sr, sr, -sc_, sc_] * (LANES // HEAD_DIM), axis=1)
    return cos, sin


def kernel(x, c, ctx, c_ctx, w_mod, b_mod, g_pre_mix, g_post_mix, w_in, q_norm_g, k_norm_g, lambda_q1, lambda_k1,
           lambda_q2, lambda_k2, subln_g, w_out, g_pre_ffn, g_post_ffn, w_router, router_bias, w_exp_gate, w_exp_up,
           w_exp_down, w_sh_gate, w_sh_up, w_sh_down):
    assert w_mod.shape[0] == 1, "single-layer block"
    b, s, d = x.shape
    sc = ctx.shape[1]
    sk = s + sc
    ne = w_router.shape[2]
    assert w_in.shape[2] == IN_WIDTH and ne % N_GROUPS == 0 and ne <= LANES
    tl = _tiles(s, sc)

    pad = (-(b + 1)) % 8
    cc = jnp.concatenate([c, c_ctx[None], jnp.zeros((pad, d), f32)], axis=0)
    mod_all = _mod(cc, w_mod[0], b_mod[0])
    mod = mod_all[:b].reshape(b, N_MOD, d)
    mod_c = mod_all[b:b + 1].reshape(1, N_MOD, d)

    w_in_b = w_in[0].astype(bf16)
    qg = jnp.tile(q_norm_g[0], GQA_HEADS)[None]
    kg = jnp.tile(k_norm_g[0], GQA_KV_HEADS)[None]
    head_id = jnp.arange(GQ_W) // HEAD_DIM
    bd = (head_id[:, None] == head_id[None, :]).astype(bf16)
    cos, sin = _rope_tables(s)
    g_pre = g_pre_mix[0][None]

    dq, gq, dk, dv, gk, gv = _inproj(x, mod, g_pre, w_in_b, qg, kg, bd, cos, sin, None,
                                     latent=True, sk=sk, row_off=0, tm=tl["tm"])
    dk, dv, gk, gv = _inproj(ctx, mod_c, g_pre, w_in_b, qg, kg, bd, cos, sin, (dk, dv, gk, gv),
                             latent=False, sk=sk, row_off=s, tm=tl["tm_ctx"])

    lamv = jnp.stack([lambda_q1[0], lambda_k1[0], lambda_q2[0], lambda_k2[0]])
    mixed = _attention(dq, gq, dk, dv, gk, gv, lamv, subln_g[0][None], tq=tl["tq"])

    x1, h2, st = _outproj(mixed, x, mod, w_out[0].astype(bf16), g_post_mix[0][None], g_pre_ffn[0][None],
                          w_router[0].T.astype(bf16), tm=tl["tm"])
    t = b * s
    nt = t // SUB
    assert s % SUB == 0 and ne <= N_EXPERTS_MAX
    cnt, posl, gatel = _route(st, router_bias[0][:, None])
    rows_cap = -(-(t * TOP_K + nt * ne * ROW_ALIGN + ne * (GMM_BLOCK - 1)) // GMM_BLOCK) * GMM_BLOCK
    meta, be, nv, nb = _dispatch_tables(cnt, nt, ne, rows_cap)
    h2f = h2.reshape(t, d)
    xs = _dispatch(meta, posl, h2f, rows_cap)
    ys = _gmm(be, nv, nb, xs, w_exp_gate[0], w_exp_up[0], w_exp_down[0])
    out = _combine(meta, posl, gatel, h2f, w_sh_gate[0].astype(bf16), w_sh_up[0].astype(bf16), w_sh_down[0].astype(bf16),
                   x1.reshape(t, d), mod, g_post_ffn[0][None], ys, s=s)
    return out.reshape(b, s, d)
```

```python
import functools
import math

import jax
import jax.numpy as jnp
from jax import lax
from jax.experimental import pallas as pl
from jax.experimental.pallas import tpu as pltpu

HEAD_DIM = 64
DIFF_HEADS = 4
GQA_HEADS = 8
GQA_KV_HEADS = 2
GRID_W = 64
ROPE_THETA = 10000.0
N_GROUPS = 8
TOPK_GROUPS = 4
TOP_K = 8
ROUTED_SCALE = 2.5
N_MOD = 6
EPS = 1e-6
LAM_INIT = 0.8 - 0.6 * math.exp(-0.3 * 0)

LANES = 128
DQ_W = DIFF_HEADS * 2 * HEAD_DIM
DV_W = DIFF_HEADS * 2 * HEAD_DIM
GQ_W = GQA_HEADS * HEAD_DIM
GK_W = GQA_KV_HEADS * HEAD_DIM
OFF_DQ, OFF_DK, OFF_DV = 0, DQ_W, 2 * DQ_W
OFF_GQ = OFF_DV + DV_W
OFF_GK = OFF_GQ + GQ_W
OFF_GV = OFF_GK + GK_W
IN_WIDTH = OFF_GV + GK_W

VMEM_LIMIT = 56 * 1024 * 1024

N_EXPERTS_MAX = 64
SUB = 256
ROW_ALIGN = 16
GMM_BLOCK = 2048
GMM_RING = 3
NLOC = -(-(SUB * TOP_K + N_EXPERTS_MAX * ROW_ALIGN) // SUB) * SUB
NLOC_MIN = SUB * TOP_K // SUB * SUB


def _swiglu_hidden(h, wg, wu):
    return jax.nn.silu(jnp.dot(h, wg, preferred_element_type=f32)) * jnp.dot(h, wu, preferred_element_type=f32)

f32 = jnp.float32
bf16 = jnp.bfloat16
NT_DIMS = (((1,), (1,)), ((), ()))
TN_DIMS = (((0,), (0,)), ((), ()))


def _cparams(*sem):
    return pltpu.CompilerParams(dimension_semantics=sem, vmem_limit_bytes=VMEM_LIMIT)


def _rms(x):
    return x * lax.rsqrt(jnp.mean(x * x, axis=-1, keepdims=True) + EPS)


def _mod_kernel(c_ref, w_ref, b_ref, o_ref):
    o_ref[...] = jnp.dot(jax.nn.silu(c_ref[...]), w_ref[...], preferred_element_type=f32) + b_ref[...]


def _mod(cc, w_mod, b_mod):
    rows, d = cc.shape
    n = w_mod.shape[1]
    bn = d
    return pl.pallas_call(
        _mod_kernel,
        grid=(n // bn,),
        in_specs=[pl.BlockSpec((rows, d), lambda j: (0, 0)),
                  pl.BlockSpec((d, bn), lambda j: (0, j)),
                  pl.BlockSpec((1, bn), lambda j: (0, j))],
        out_specs=pl.BlockSpec((rows, bn), lambda j: (0, j)),
        out_shape=jax.ShapeDtypeStruct((rows, n), f32),
        compiler_params=_cparams("arbitrary"),
        name="mod",
    )(cc, w_mod, b_mod.reshape(1, n))


def _rope_tile(xt, cos, sin_signed, lane_lo):
    partner = jnp.where(lane_lo, pltpu.roll(xt, LANES - 16, 1), pltpu.roll(xt, 16, 1))
    return xt * cos + partner * sin_signed


def _inproj_kernel(x_ref, mod_ref, g_ref, w_ref, qg_ref, kg_ref, bd_ref, cos_ref, sin_ref, *out_refs,
                   latent):
    x = x_ref[0]
    h = _rms(x) * g_ref[...]
    h = h * (1.0 + mod_ref[0, 1:2, :]) + mod_ref[0, 0:1, :]
    hb = h.astype(bf16)
    tm = x.shape[0]

    def proj(off, width):
        return jnp.dot(hb, w_ref[:, off:off + width], preferred_element_type=f32)

    def head_rms(v, gain):
        w = v.shape[1]
        ss = jnp.dot((v * v).astype(bf16), bd_ref[0:w, 0:w], preferred_element_type=f32)
        return v * lax.rsqrt(ss * (1.0 / HEAD_DIM) + EPS) * gain

    if latent:
        dq_ref, gq_ref, dk_ref, dv_ref, gk_ref, gv_ref = out_refs
        cos, sin = cos_ref[...], sin_ref[...]
        lane_lo = (lax.broadcasted_iota(jnp.int32, (tm, LANES), 1) % 32) < 16
        rope = lambda v: _rope_tile(v, cos, sin, lane_lo)
    else:
        dk_ref, dv_ref, gk_ref, gv_ref = out_refs[-4:]
        rope = lambda v: v

    def store(ref, v, fn):
        for t in range(v.shape[1] // LANES):
            sl = slice(LANES * t, LANES * (t + 1))
            ref[0, :, sl] = fn(v[:, sl]).astype(ref.dtype)

    qscale = HEAD_DIM ** -0.5 * math.log2(math.e)
    if latent:
        store(dq_ref, proj(OFF_DQ, DQ_W) * qscale, rope)
        store(gq_ref, head_rms(proj(OFF_GQ, GQ_W), qg_ref[...] * qscale), rope)
    store(dk_ref, proj(OFF_DK, DQ_W), rope)
    store(gk_ref, head_rms(proj(OFF_GK, GK_W), kg_ref[...]), rope)
    dv_ref[0] = proj(OFF_DV, DV_W).astype(bf16)
    gv_ref[0] = proj(OFF_GV, GK_W).astype(bf16)


def _inproj(x, mod, g_pre, w_in_b, qg, kg, bd, cos, sin, kv_bufs, *, latent, sk, row_off, tm):
    b, s, d = x.shape
    nt = s // tm
    kv_blk = row_off // tm
    kv_widths = (DQ_W, DV_W, GK_W, GK_W)
    const2 = lambda bi, i: (0, 0)
    in_specs = [
        pl.BlockSpec((1, tm, d), lambda bi, i: (bi, i, 0)),
        pl.BlockSpec((1, N_MOD, d), (lambda bi, i: (bi, 0, 0)) if latent else (lambda bi, i: (0, 0, 0))),
        pl.BlockSpec((1, d), const2),
        pl.BlockSpec((d, IN_WIDTH), const2),
        pl.BlockSpec((1, GQ_W), const2),
        pl.BlockSpec((1, GK_W), const2),
        pl.BlockSpec((GQ_W, GQ_W), const2),
        pl.BlockSpec((tm, LANES), (lambda bi, i: (i, 0)) if latent else const2),
        pl.BlockSpec((tm, LANES), (lambda bi, i: (i, 0)) if latent else const2),
    ]
    kv_specs = [pl.BlockSpec((1, tm, w), lambda bi, i: (bi, kv_blk + i, 0)) for w in kv_widths]
    kv_shapes = [jax.ShapeDtypeStruct((b, sk, w), bf16) for w in kv_widths]
    args = [x, mod, g_pre, w_in_b, qg, kg, bd, cos, sin]
    if latent:
        out_specs = [pl.BlockSpec((1, tm, DQ_W), lambda bi, i: (bi, i, 0)),
                     pl.BlockSpec((1, tm, GQ_W), lambda bi, i: (bi, i, 0))] + kv_specs
        out_shape = [jax.ShapeDtypeStruct((b, s, DQ_W), bf16), jax.ShapeDtypeStruct((b, s, GQ_W), bf16)] + kv_shapes
        aliases = {}
    else:
        in_specs += [pl.BlockSpec(memory_space=pl.ANY)] * 4
        args += list(kv_bufs)
        out_specs, out_shape = kv_specs, kv_shapes
        aliases = {len(args) - 4 + k: k for k in range(4)}
    return pl.pallas_call(
        functools.partial(_inproj_kernel, latent=latent),
        grid=(b, nt),
        in_specs=in_specs,
        out_specs=out_specs,
        out_shape=out_shape,
        input_output_aliases=aliases,
        compiler_params=_cparams("parallel", "arbitrary"),
        name="inproj_latent" if latent else "inproj_ctx",
    )(*args)


def _attn_kernel(dq_ref, gq_ref, dk_ref, dv_ref, gk_ref, gv_ref, lam_ref, sg_ref, o_ref, vext_ref, s_ref, p_ref):
    tq = dq_ref.shape[1]
    sk = dk_ref.shape[1]

    @pl.when(pl.program_id(1) == 0)
    def _():
        ones = jnp.ones((sk, LANES), bf16)
        for h in range(DIFF_HEADS):
            vext_ref[h, :, :LANES] = dv_ref[0, :, LANES * h:LANES * (h + 1)]
            vext_ref[h, :, LANES:] = ones
        vext_ref[DIFF_HEADS, :, :LANES] = gv_ref[0]
        vext_ref[DIFF_HEADS, :, LANES:] = ones

    lane = lax.broadcasted_iota(jnp.int32, (tq, LANES), 1)
    lo = lane < HEAD_DIM
    lv = lam_ref[...]
    lam = (jnp.exp(jnp.sum(lv[0:1] * lv[1:2], axis=1, keepdims=True))
           - jnp.exp(jnp.sum(lv[2:3] * lv[3:4], axis=1, keepdims=True)) + LAM_INIT)
    zero = jnp.zeros((), bf16)
    group = GQA_HEADS // GQA_KV_HEADS

    maps = [("diff", h, w) for h in range(DIFF_HEADS) for w in range(2)]
    maps += [("gqa", t, pos) for t in range(GQA_HEADS // 2) for pos in range(2)]

    def scores(kind, a, b):
        if kind == "diff":
            sl = slice(LANES * a, LANES * (a + 1))
            qt = dq_ref[0, :, sl]
            qm = jnp.where(lo, qt, zero) if b == 0 else jnp.where(lo, zero, qt)
            return lax.dot_general(qm, dk_ref[0, :, sl], NT_DIMS, preferred_element_type=f32)
        j = (2 * a + b) // group
        qt = gq_ref[0, :, LANES * a:LANES * (a + 1)]
        if b != j:
            qt = pltpu.roll(qt, HEAD_DIM, 1)
        qm = jnp.where(lo if j == 0 else ~lo, qt, zero)
        return lax.dot_general(qm, gk_ref[0], NT_DIMS, preferred_element_type=f32)

    def values(i):
        kind, a, b = maps[i]
        r = jnp.dot(p_ref[i % 2], vext_ref[a if kind == "diff" else DIFF_HEADS], preferred_element_type=f32)
        o = r[:, :LANES] / r[:, LANES:]
        if kind == "gqa" and b != (2 * a + b) // group:
            o = pltpu.roll(o, HEAD_DIM, 1)
        return o

    outs = []
    s_ref[0] = scores(*maps[0])
    for i in range(len(maps)):
        if i + 1 < len(maps):
            s_ref[(i + 1) % 2] = scores(*maps[i + 1])
        s = s_ref[i % 2]
        p_ref[i % 2] = jnp.exp2(s - jnp.max(s, axis=-1, keepdims=True)).astype(bf16)
        outs.append(values(i))

    for h in range(DIFF_HEADS):
        o = _rms(outs[2 * h] - lam * outs[2 * h + 1]) * (sg_ref[...] * (1.0 - LAM_INIT))
        o_ref[0, :, LANES * h:LANES * (h + 1)] = o.astype(o_ref.dtype)
    for t in range(GQA_HEADS // 2):
        base = 2 * DIFF_HEADS + 2 * t
        off = DIFF_HEADS * 2 * HEAD_DIM + LANES * t
        o_ref[0, :, off:off + LANES] = jnp.where(lo, outs[base], outs[base + 1]).astype(o_ref.dtype)


def _attention(dq, gq, dk, dv, gk, gv, lamv, subln_g, *, tq):
    b, s, _ = dq.shape
    sk = dk.shape[1]
    qspec = lambda w: pl.BlockSpec((1, tq, w), lambda bi, qi: (bi, qi, 0))
    kspec = lambda w: pl.BlockSpec((1, sk, w), lambda bi, qi: (bi, 0, 0))
    const2 = lambda bi, qi: (0, 0)
    mix_w = DV_W + GQ_W
    return pl.pallas_call(
        _attn_kernel,
        grid=(b, s // tq),
        in_specs=[qspec(DQ_W), qspec(GQ_W), kspec(DQ_W), kspec(DV_W), kspec(GK_W), kspec(GK_W),
                  pl.BlockSpec((4, HEAD_DIM), const2), pl.BlockSpec((1, 2 * HEAD_DIM), const2)],
        out_specs=pl.BlockSpec((1, tq, mix_w), lambda bi, qi: (bi, qi, 0)),
        out_shape=jax.ShapeDtypeStruct((b, s, mix_w), bf16),
        scratch_shapes=[pltpu.VMEM((DIFF_HEADS + 1, sk, 2 * LANES), bf16),
                        pltpu.VMEM((2, tq, sk), f32), pltpu.VMEM((2, tq, sk), bf16)],
        compiler_params=_cparams("parallel", "arbitrary"),
        name="attn",
    )(dq, gq, dk, dv, gk, gv, lamv, subln_g)


def _outproj_kernel(mix_ref, x_ref, mod_ref, wo_ref, gpost_ref, gpre_ref, wr_ref, x1_ref, h2_ref, st_ref):
    o = jnp.dot(mix_ref[0], wo_ref[...], preferred_element_type=f32)
    x1 = x_ref[0] + mod_ref[0, 2:3, :] * (_rms(o) * gpost_ref[...])
    x1_ref[0] = x1
    h2 = _rms(x1) * gpre_ref[...]
    h2 = (h2 * (1.0 + mod_ref[0, 4:5, :]) + mod_ref[0, 3:4, :]).astype(bf16)
    h2_ref[0] = h2
    st_ref[...] = jax.nn.sigmoid(lax.dot_general(wr_ref[...], h2, NT_DIMS, preferred_element_type=f32))


def _outproj(mixed, x, mod, w_out_b, g_post, g_pre, wr_t, *, tm):
    b, s, d = x.shape
    nt = s // tm
    ne = wr_t.shape[0]
    mw = mixed.shape[2]
    const2 = lambda bi, i: (0, 0)
    tile = lambda w: pl.BlockSpec((1, tm, w), lambda bi, i: (bi, i, 0))
    return pl.pallas_call(
        _outproj_kernel,
        grid=(b, nt),
        in_specs=[tile(mw), tile(d), pl.BlockSpec((1, N_MOD, d), lambda bi, i: (bi, 0, 0)),
                  pl.BlockSpec((mw, d), const2), pl.BlockSpec((1, d), const2), pl.BlockSpec((1, d), const2),
                  pl.BlockSpec((ne, d), const2)],
        out_specs=[tile(d), tile(d), pl.BlockSpec((ne, tm), lambda bi, i: (0, bi * nt + i))],
        out_shape=[jax.ShapeDtypeStruct((b, s, d), f32), jax.ShapeDtypeStruct((b, s, d), bf16),
                   jax.ShapeDtypeStruct((ne, b * s), f32)],
        compiler_params=_cparams("parallel", "arbitrary"),
        name="outproj",
    )(mixed, x, mod, w_out_b, g_post, g_pre, wr_t)


def _first_argmax(v, iota, n):
    m = jnp.max(v, axis=0, keepdims=True)
    idx = jnp.min(jnp.where(v == m, iota, n), axis=0, keepdims=True)
    return m, idx


def _route_gates_t(s, bias):
    ne, tm = s.shape
    gsz = ne // N_GROUPS
    neg = -jnp.inf
    ssel = s + bias
    sub = lax.broadcasted_iota(jnp.int32, (gsz, tm), 0).astype(f32)
    gscore = []
    for g in range(N_GROUPS):
        blk = ssel[gsz * g:gsz * (g + 1)]
        m1, i1 = _first_argmax(blk, sub, float(gsz))
        m2 = jnp.max(jnp.where(sub == i1, neg, blk), axis=0, keepdims=True)
        gscore.append(m1 + m2)
    gs = jnp.concatenate(gscore, axis=0)
    giota = lax.broadcasted_iota(jnp.int32, (N_GROUPS, tm), 0).astype(f32)
    gsel = jnp.zeros((N_GROUPS, tm), f32)
    for _ in range(TOPK_GROUPS):
        _, gi = _first_argmax(gs, giota, float(N_GROUPS))
        hit = giota == gi
        gsel = jnp.where(hit, 1.0, gsel)
        gs = jnp.where(hit, neg, gs)
    emask = jnp.concatenate([jnp.broadcast_to(gsel[g:g + 1], (gsz, tm)) for g in range(N_GROUPS)], axis=0)
    cand = jnp.where(emask > 0.0, ssel, neg)
    eiota = lax.broadcasted_iota(jnp.int32, (ne, tm), 0).astype(f32)
    w = jnp.zeros((ne, tm), f32)
    sel = jnp.zeros((ne, tm), f32)
    for _ in range(TOP_K):
        _, ei = _first_argmax(cand, eiota, float(ne))
        hit = eiota == ei
        w = jnp.where(hit, s, w)
        sel = jnp.where(hit, 1.0, sel)
        cand = jnp.where(hit, neg, cand)
    return w / jnp.sum(w, axis=0, keepdims=True) * ROUTED_SCALE, sel


def _route_kernel(st_ref, bias_ref, cnt_ref, posl_ref, gate_ref):
    gates, sel = _route_gates_t(st_ref[...], bias_ref[...])
    ne, sub = sel.shape
    n = jnp.sum(sel, axis=1, keepdims=True)
    cnt_ref[...] = jnp.broadcast_to(n, (ne, LANES))
    c = jnp.maximum(jnp.floor((n + (ROW_ALIGN - 1)) * (1.0 / ROW_ALIGN)), 1.0) * ROW_ALIGN
    lower = (lax.broadcasted_iota(jnp.int32, (ne, ne), 1) < lax.broadcasted_iota(jnp.int32, (ne, ne), 0)).astype(f32)
    base = jnp.dot(lower, jnp.broadcast_to(c, (ne, LANES)), preferred_element_type=f32)[:, 0:1]
    upper = (lax.broadcasted_iota(jnp.int32, (sub, sub), 0) < lax.broadcasted_iota(jnp.int32, (sub, sub), 1)).astype(bf16)
    rank = jnp.dot(sel.astype(bf16), upper, preferred_element_type=f32)
    pos = base + rank
    eiota = lax.broadcasted_iota(jnp.int32, (ne, sub), 0).astype(f32)
    rem = sel
    pos_rows, w_rows = [], []
    for _ in range(TOP_K):
        ei = jnp.min(jnp.where(rem > 0.0, eiota, float(ne)), axis=0, keepdims=True)
        hit = eiota == ei
        pos_rows.append(jnp.sum(jnp.where(hit, pos, 0.0), axis=0, keepdims=True))
        w_rows.append(jnp.sum(jnp.where(hit, gates, 0.0), axis=0, keepdims=True))
        rem = jnp.where(hit, 0.0, rem)
    posl_ref[0] = jnp.concatenate(pos_rows, axis=0)
    gate_ref[0] = jnp.concatenate(w_rows, axis=0)


def _route(st, bias):
    ne, t = st.shape
    nt = t // SUB
    return pl.pallas_call(
        _route_kernel,
        grid=(nt,),
        in_specs=[pl.BlockSpec((ne, SUB), lambda i: (0, i)), pl.BlockSpec((ne, 1), lambda i: (0, 0))],
        out_specs=[pl.BlockSpec((ne, LANES), lambda i: (i, 0)),
                   pl.BlockSpec((1, TOP_K, SUB), lambda i: (i, 0, 0)),
                   pl.BlockSpec((1, TOP_K, SUB), lambda i: (i, 0, 0))],
        out_shape=[jax.ShapeDtypeStruct((nt * ne, LANES), f32),
                   jax.ShapeDtypeStruct((nt, TOP_K, SUB), f32),
                   jax.ShapeDtypeStruct((nt, TOP_K, SUB), f32)],
        compiler_params=_cparams("parallel"),
        name="route",
    )(st, bias)


def _rows_copy(loc_ref, hbm_ref, sem, slot, lo, go, rows, to_hbm):
    src = loc_ref.at[slot, pl.ds(lo, rows)]
    dst = hbm_ref.at[pl.ds(go, rows)]
    return pltpu.make_async_copy(src, dst, sem.at[slot]) if to_hbm else pltpu.make_async_copy(dst, src, sem.at[slot])


def _start_chunks(meta_ref, loc_ref, hbm_ref, sem, slot, to_hbm):
    def per_expert(e, base):
        c = pl.multiple_of(meta_ref[0, 0, e], ROW_ALIGN)
        g = pl.multiple_of(meta_ref[0, 1, e], ROW_ALIGN)
        _rows_copy(loc_ref, hbm_ref, sem, slot, pl.multiple_of(base, ROW_ALIGN), g, c, to_hbm).start()
        return base + c

    return lax.fori_loop(0, meta_ref.shape[2], per_expert, 0, unroll=8)


def _wait_rows(rows, loc_ref, hbm_ref, sem, slot, to_hbm):
    units = rows // ROW_ALIGN
    for bit in range((NLOC // ROW_ALIGN).bit_length()):
        @pl.when((units >> bit) & 1 == 1)
        def _():
            _rows_copy(loc_ref, hbm_ref, sem, slot, 0, 0, ROW_ALIGN << bit, to_hbm).wait()


def _select_chunk(posl, vals, r, sub):
    rel = posl - float(r * sub)
    rel = jnp.where((rel >= 0.0) & (rel < float(sub)), rel, -1.0).astype(bf16)
    piota = lax.broadcasted_iota(jnp.int32, (sub, sub), 0).astype(f32).astype(bf16)
    m = jnp.zeros((sub, sub), bf16)
    for k in range(TOP_K):
        val = jnp.ones((), bf16) if vals is None else vals[k:k + 1]
        m = jnp.where(piota == rel[k:k + 1], val, m)
    return m


def _dispatch_kernel(meta_ref, posl_ref, h_ref, xs_ref, loc_ref, sem, nrows):
    i = pl.program_id(0)
    last = pl.num_programs(0) - 1
    slot = i % 2
    sub = h_ref.shape[0]

    @pl.when(i >= 2)
    def _():
        _wait_rows(nrows[slot], loc_ref, xs_ref, sem, slot, True)

    def tile_rows(e, acc):
        return acc + meta_ref[0, 0, e]

    tot = lax.fori_loop(0, meta_ref.shape[2], tile_rows, 0, unroll=8)
    h = h_ref[...]
    posl = posl_ref[0]

    def sort_chunk(r):
        rows = jnp.dot(_select_chunk(posl, None, r, sub), h, preferred_element_type=f32)
        loc_ref[slot, r * sub:(r + 1) * sub, :] = rows.astype(loc_ref.dtype)

    always = (NLOC_MIN + NLOC) // (2 * sub)
    for r in range(always):
        sort_chunk(r)
    for r in range(always, NLOC // sub):
        pl.when(r * sub < tot)(functools.partial(sort_chunk, r))

    nrows[slot] = _start_chunks(meta_ref, loc_ref, xs_ref, sem, slot, True)

    @pl.when(i == last)
    def _():
        _wait_rows(nrows[slot], loc_ref, xs_ref, sem, slot, True)

        @pl.when(i >= 1)
        def _():
            _wait_rows(nrows[1 - slot], loc_ref, xs_ref, sem, 1 - slot, True)


def _dispatch(meta, posl, h2, rows_cap):
    t, d = h2.shape
    nt = t // SUB
    ne = meta.shape[2]
    return pl.pallas_call(
        _dispatch_kernel,
        grid=(nt,),
        in_specs=[pl.BlockSpec((1, 2, ne), lambda i: (i, 0, 0), memory_space=pltpu.SMEM),
                  pl.BlockSpec((1, TOP_K, SUB), lambda i: (i, 0, 0)),
                  pl.BlockSpec((SUB, d), lambda i: (i, 0))],
        out_specs=pl.BlockSpec(memory_space=pl.ANY),
        out_shape=jax.ShapeDtypeStruct((rows_cap, d), bf16),
        scratch_shapes=[pltpu.VMEM((2, NLOC, d), bf16), pltpu.SemaphoreType.DMA((2,)), pltpu.SMEM((2,), jnp.int32)],
        compiler_params=_cparams("arbitrary"),
        name="dispatch",
    )(meta, posl, h2)


def _gmm_kernel(be_ref, nv_ref, nb_ref, x_hbm, wg_ref, wu_ref, wd_ref, y_ref, xbuf, sem):
    i = pl.program_id(0)
    nv = nv_ref[i]
    nb = nb_ref[0]

    def row_copy(blk, slot):
        src = x_hbm.at[pl.ds(pl.multiple_of(blk * GMM_BLOCK, GMM_BLOCK), GMM_BLOCK)]
        return pltpu.make_async_copy(src, xbuf.at[slot], sem.at[slot])

    @pl.when(i == 0)
    def _():
        for k in range(GMM_RING - 1):
            pl.when(k < nb)(lambda k=k: row_copy(k, k).start())

    ahead = i + (GMM_RING - 1)

    @pl.when(ahead < nb)
    def _():
        row_copy(ahead, ahead % GMM_RING).start()

    @pl.when(i < nb)
    def _():
        slot = i % GMM_RING
        row_copy(i, slot).wait()
        x = xbuf[slot]
        row = lax.broadcasted_iota(jnp.int32, x.shape, 0)
        x = jnp.where(row < nv, x, jnp.zeros((), x.dtype))
        a = _swiglu_hidden(x, wg_ref[0].astype(bf16), wu_ref[0].astype(bf16))
        y_ref[...] = jnp.dot(a.astype(bf16), wd_ref[0].astype(bf16), preferred_element_type=f32).astype(y_ref.dtype)


def _gmm(be, nv, nb, xs, weg, weu, wed):
    rows_cap, d = xs.shape
    de = weg.shape[2]
    blk = lambda i, be, nv, nb: (jnp.minimum(i, nb[0] - 1), 0)
    wsel = lambda i, be, nv, nb: (be[i], 0, 0)
    return pl.pallas_call(
        _gmm_kernel,
        grid_spec=pltpu.PrefetchScalarGridSpec(
            num_scalar_prefetch=3,
            grid=(rows_cap // GMM_BLOCK,),
            in_specs=[pl.BlockSpec(memory_space=pl.ANY),
                      pl.BlockSpec((1, d, de), wsel), pl.BlockSpec((1, d, de), wsel), pl.BlockSpec((1, de, d), wsel)],
            out_specs=pl.BlockSpec((GMM_BLOCK, d), blk),
            scratch_shapes=[pltpu.VMEM((GMM_RING, GMM_BLOCK, d), bf16), pltpu.SemaphoreType.DMA((GMM_RING,))]),
        out_shape=jax.ShapeDtypeStruct((rows_cap, d), bf16),
        compiler_params=_cparams("arbitrary"),
        name="gmm",
    )(be, nv, nb, xs, weg, weu, wed)


def _combine_kernel(meta_ref, metan_ref, posl_ref, gate_ref, h_ref, wsg_ref, wsu_ref, wsd_ref, x1_ref, mod_ref, gp_ref,
                    ys_ref, o_ref, loc_ref, w_ref, sem, nrows):
    i = pl.program_id(0)
    last = pl.num_programs(0) - 1
    slot = i % 2
    sub = h_ref.shape[0]

    @pl.when(i == 0)
    def _():
        loc_ref[...] = jnp.zeros(loc_ref.shape, loc_ref.dtype)
        nrows[0] = _start_chunks(meta_ref, loc_ref, ys_ref, sem, 0, False)

    @pl.when(i < last)
    def _():
        nrows[1 - slot] = _start_chunks(metan_ref, loc_ref, ys_ref, sem, 1 - slot, False)

    _wait_rows(nrows[slot], loc_ref, ys_ref, sem, slot, False)

    h = h_ref[...]
    a = _swiglu_hidden(h, wsg_ref[...], wsu_ref[...])
    y = jnp.dot(a.astype(bf16), wsd_ref[...], preferred_element_type=f32)

    posl = posl_ref[0]
    gates = gate_ref[0].astype(bf16)
    for r in range(NLOC // sub):
        w_ref[r * sub:(r + 1) * sub, :] = _select_chunk(posl, gates, r, sub)
    y = y + lax.dot_general(w_ref[...], loc_ref[slot], TN_DIMS, preferred_element_type=f32)
    o_ref[...] = x1_ref[...] + mod_ref[0, 5:6, :] * (_rms(y) * gp_ref[...])


def _combine(meta, posl, gatel, h2, wsg, wsu, wsd, x1, mod, g_post, ys, *, s):
    t, d = h2.shape
    nt = t // SUB
    ne = meta.shape[2]
    ds = wsg.shape[1]
    per_b = s // SUB
    tile = lambda w: pl.BlockSpec((SUB, w), lambda i: (i, 0))
    const2 = lambda i: (0, 0)
    return pl.pallas_call(
        _combine_kernel,
        grid=(nt,),
        in_specs=[pl.BlockSpec((1, 2, ne), lambda i: (i, 0, 0), memory_space=pltpu.SMEM),
                  pl.BlockSpec((1, 2, ne), lambda i: (jnp.minimum(i + 1, nt - 1), 0, 0), memory_space=pltpu.SMEM),
                  pl.BlockSpec((1, TOP_K, SUB), lambda i: (i, 0, 0)), pl.BlockSpec((1, TOP_K, SUB), lambda i: (i, 0, 0)),
                  tile(d),
                  pl.BlockSpec((d, ds), const2), pl.BlockSpec((d, ds), const2), pl.BlockSpec((ds, d), const2),
                  tile(d), pl.BlockSpec((1, N_MOD, d), lambda i: (i // per_b, 0, 0)), pl.BlockSpec((1, d), const2),
                  pl.BlockSpec(memory_space=pl.ANY)],
        out_specs=tile(d),
        out_shape=jax.ShapeDtypeStruct((t, d), f32),
        scratch_shapes=[pltpu.VMEM((2, NLOC, d), bf16), pltpu.VMEM((NLOC, SUB), bf16),
                        pltpu.SemaphoreType.DMA((2,)), pltpu.SMEM((2,), jnp.int32)],
        compiler_params=_cparams("arbitrary"),
        name="combine",
    )(meta, meta, posl, gatel, h2, wsg, wsu, wsd, x1, mod, g_post, ys)


def _tiles(s, sc):
    tm = math.gcd(s, 1024)
    assert sc % tm == 0 or tm % sc == 0
    return dict(tm=tm, tm_ctx=math.gcd(sc, tm), tq=math.gcd(s, 512))


def _dispatch_tables(cnt, nt, ne, rows_cap):
    n = cnt.reshape(nt, ne, LANES)[:, :, 0].astype(jnp.int32)
    c = jnp.maximum((n + (ROW_ALIGN - 1)) // ROW_ALIGN, 1) * ROW_ALIGN
    n_e = jnp.sum(c, axis=0)
    p_e = (n_e + (GMM_BLOCK - 1)) // GMM_BLOCK * GMM_BLOCK
    seg_end = jnp.cumsum(p_e)
    seg_start = seg_end - p_e
    g = seg_start[None, :] + jnp.cumsum(c, axis=0) - c
    meta = jnp.stack([c, g], axis=1)
    blk_start = jnp.arange(rows_cap // GMM_BLOCK, dtype=jnp.int32) * GMM_BLOCK
    be = jnp.minimum(jnp.sum(blk_start[:, None] >= seg_end[None, :], axis=1), ne - 1).astype(jnp.int32)
    nv = jnp.clip(n_e[be] - (blk_start - seg_start[be]), 0, GMM_BLOCK).astype(jnp.int32)
    nb = jnp.maximum(seg_end[-1:] // GMM_BLOCK, 1).astype(jnp.int32)
    return meta, be, nv, nb


def _rope_tables(s):
    n_freq = HEAD_DIM // 4
    inv_freq = ROPE_THETA ** (-jnp.arange(n_freq, dtype=f32) / n_freq)
    pos = jnp.arange(s)
    ang_r = (pos // GRID_W).astype(f32)[:, None] * inv_freq
    ang_c = (pos % GRID_W).astype(f32)[:, None] * inv_freq
    cr, sr, cc, sc_ = jnp.cos(ang_r), jnp.sin(ang_r), jnp.cos(ang_c), jnp.sin(ang_c)
    cos = jnp.concatenate([cr, cr, cc, cc] * (LANES // HEAD_DIM), axis=1)
    sin = jnp.concatenate([-sr, sr, -sc_, sc_] * (LANES // HEAD_DIM), axis=1)
    return cos, sin


def kernel(x, c, ctx, c_ctx, w_mod, b_mod, g_pre_mix, g_post_mix, w_in, q_norm_g, k_norm_g, lambda_q1, lambda_k1,
           lambda_q2, lambda_k2, subln_g, w_out, g_pre_ffn, g_post_ffn, w_router, router_bias, w_exp_gate, w_exp_up,
           w_exp_down, w_sh_gate, w_sh_up, w_sh_down):
    assert w_mod.shape[0] == 1, "single-layer block"
    b, s, d = x.shape
    sc = ctx.shape[1]
    sk = s + sc
    ne = w_router.shape[2]
    assert w_in.shape[2] == IN_WIDTH and ne % N_GROUPS == 0 and ne <= LANES
    tl = _tiles(s, sc)

    pad = (-(b + 1)) % 8
    cc = jnp.concatenate([c, c_ctx[None], jnp.zeros((pad, d), f32)], axis=0)
    mod_all = _mod(cc, w_mod[0], b_mod[0])
    mod = mod_all[:b].reshape(b, N_MOD, d)
    mod_c = mod_all[b:b + 1].reshape(1, N_MOD, d)

    w_in_b = w_in[0].astype(bf16)
    qg = jnp.tile(q_norm_g[0], GQA_HEADS)[None]
    kg = jnp.tile(k_norm_g[0], GQA_KV_HEADS)[None]
    head_id = jnp.arange(GQ_W) // HEAD_DIM
    bd = (head_id[:, None] == head_id[None, :]).astype(bf16)
    cos, sin = _rope_tables(s)
    g_pre = g_pre_mix[0][None]

    dq, gq, dk, dv, gk, gv = _inproj(x, mod, g_pre, w_in_b, qg, kg, bd, cos, sin, None,
                                     latent=True, sk=sk, row_off=0, tm=tl["tm"])
    dk, dv, gk, gv = _inproj(ctx, mod_c, g_pre, w_in_b, qg, kg, bd, cos, sin, (dk, dv, gk, gv),
                             latent=False, sk=sk, row_off=s, tm=tl["tm_ctx"])

    lamv = jnp.stack([lambda_q1[0], lambda_k1[0], lambda_q2[0], lambda_k2[0]])
    mixed = _attention(dq, gq, dk, dv, gk, gv, lamv, subln_g[0][None], tq=tl["tq"])

    x1, h2, st = _outproj(mixed, x, mod, w_out[0].astype(bf16), g_post_mix[0][None], g_pre_ffn[0][None],
                          w_router[0].T.astype(bf16), tm=tl["tm"])
    t = b * s
    nt = t // SUB
    assert s % SUB == 0 and ne <= N_EXPERTS_MAX
    cnt, posl, gatel = _route(st, router_bias[0][:, None])
    rows_cap = -(-(t * TOP_K + nt * ne * ROW_ALIGN + ne * (GMM_BLOCK - 1)) // GMM_BLOCK) * GMM_BLOCK
    meta, be, nv, nb = _dispatch_tables(cnt, nt, ne, rows_cap)
    h2f = h2.reshape(t, d)
    xs = _dispatch(meta, posl, h2f, rows_cap)
    ys = _gmm(be, nv, nb, xs, w_exp_gate[0], w_exp_up[0], w_exp_down[0])
    out = _combine(meta, posl, gatel, h2f, w_sh_gate[0].astype(bf16), w_sh_up[0].astype(bf16), w_sh_down[0].astype(bf16),
                   x1.reshape(t, d), mod, g_post_ffn[0][None], ys, s=s)
    return out.reshape(b, s, d)
```
